```python
import math
import jax, jax.numpy as jnp
from jax import lax
import numpy as np

D_MODEL = 1024
BATCH = 8
SEQ = 4096
DEPTH = 1

SSM_WIDTH = 512
SSM_GROUP = 16
SSM_GROUPS = SSM_WIDTH // SSM_GROUP
SSM_STATE = 64
DT_MIN = 1e-3
DT_MAX = 1e-1
SGU_WIDTH = 512
SGU_GROUPS = 8
SGU_GROUP_DIM = SGU_WIDTH // SGU_GROUPS
CHUNK = 128
N_BRANCH = 2
IN_COLS = SSM_WIDTH + 2 * SGU_WIDTH + N_BRANCH * D_MODEL
D_FF = 2816
CONV_WIDTH = 3
EPS = 1e-6

kernel_name = "hybrid_s5_gmlp_gated_block"


def rms_norm(x, g):
    x32 = x.astype(jnp.float32)
    y = x32 * lax.rsqrt(jnp.mean(x32 * x32, axis=-1, keepdims=True) + EPS)
    return (y * g.astype(jnp.float32)).astype(x.dtype)


def s5_branch(u, a_re, a_im, log_dt, b_re, b_im, c_re, c_im, d_skip, w_glu, b_glu):
    bsz, s, _ = u.shape
    f32 = jnp.float32
    u32 = u.astype(f32).reshape(bsz, s, SSM_GROUPS, SSM_GROUP)
    dt = jnp.exp(log_dt.astype(f32))[:, None]
    ar = a_re.astype(f32)
    ai = a_im.astype(f32)
    mag = jnp.exp(dt * ar)
    abar_re = mag * jnp.cos(dt * ai)
    abar_im = mag * jnp.sin(dt * ai)
    den = ar * ar + ai * ai
    nr = abar_re - 1.0
    ni = abar_im
    f_re = (nr * ar + ni * ai) / den
    f_im = (ni * ar - nr * ai) / den
    br = b_re.astype(f32)
    bi = b_im.astype(f32)
    bbar_re = f_re[..., None] * br - f_im[..., None] * bi
    bbar_im = f_re[..., None] * bi + f_im[..., None] * br
    bu_re = jnp.einsum("bsgh,gph->bsgp", u32, bbar_re)
    bu_im = jnp.einsum("bsgh,gph->bsgp", u32, bbar_im)
    a_seq_re = jnp.broadcast_to(abar_re[None, None], (1, s, SSM_GROUPS, SSM_STATE))
    a_seq_im = jnp.broadcast_to(abar_im[None, None], (1, s, SSM_GROUPS, SSM_STATE))

    def combine(left, right):
        alr, ali, blr, bli = left
        arr, ari, brr, bri = right
        return (arr * alr - ari * ali,
                arr * ali + ari * alr,
                arr * blr - ari * bli + brr,
                arr * bli + ari * blr + bri)

    _, _, st_re, st_im = lax.associative_scan(
        combine, (a_seq_re, a_seq_im, bu_re, bu_im), axis=1)
    y = (jnp.einsum("bsgp,ghp->bsgh", st_re, c_re.astype(f32))
         - jnp.einsum("bsgp,ghp->bsgh", st_im, c_im.astype(f32)))
    y = y + d_skip.astype(f32).reshape(SSM_GROUPS, SSM_GROUP) * u32
    y = jax.nn.gelu(y.reshape(bsz, s, SSM_WIDTH))
    y = y * jax.nn.sigmoid(y @ w_glu.astype(f32) + b_glu.astype(f32))
    return y.astype(u.dtype)


def sgu_branch(uv, g_sgu, w_s, b_s):
    bsz, s, _ = uv.shape
    uv = jax.nn.gelu(uv)
    u, v = jnp.split(uv, 2, axis=-1)
    v = rms_norm(v, g_sgu)
    n_chunks = s // CHUNK
    v = v.reshape(bsz, n_chunks, CHUNK, SGU_GROUPS, SGU_GROUP_DIM)
    mask = jnp.tril(jnp.ones((CHUNK, CHUNK), dtype=bool))
    ws = jnp.where(mask[None], w_s, 0.0)
    mixed = jnp.einsum("gts,bcsgd->bctgd", ws, v) + b_s.T[:, :, None]
    mixed = mixed.reshape(bsz, s, SGU_WIDTH)
    return u * mixed


def causal_depthwise_conv(y, w, b):
    s = y.shape[1]
    yp = jnp.pad(y, ((0, 0), (CONV_WIDTH - 1, 0), (0, 0)))
    return sum(yp[:, k:k + s] * w[k] for k in range(CONV_WIDTH)) + b


def setup_inputs(seed: int = 0) -> dict:
    key = jax.random.key(seed)
    ks = jax.random.split(key, 25)
    L = DEPTH
    f32 = jnp.float32

    def nrm(k, shape, scale):
        return jax.random.normal(k, shape, f32) * scale

    n_idx = jnp.arange(SSM_STATE, dtype=f32)
    return {
        "x": nrm(ks[0], (BATCH, SEQ, D_MODEL), 1.0),
        "g_mix": 1.0 + nrm(ks[1], (L, D_MODEL), 0.02),
        "w_in": nrm(ks[2], (L, D_MODEL, IN_COLS), D_MODEL ** -0.5),
        "a_re": -0.5 + nrm(ks[3], (L, SSM_GROUPS, SSM_STATE), 0.01),
        "a_im": math.pi * n_idx + nrm(ks[4], (L, SSM_GROUPS, SSM_STATE), 0.01),
        "log_dt": jax.random.uniform(ks[5], (L, SSM_GROUPS), f32,
                                     minval=math.log(DT_MIN), maxval=math.log(DT_MAX)),
        "b_re": nrm(ks[6], (L, SSM_GROUPS, SSM_STATE, SSM_GROUP), (2 * SSM_GROUP) ** -0.5),
        "b_im": nrm(ks[7], (L, SSM_GROUPS, SSM_STATE, SSM_GROUP), (2 * SSM_GROUP) ** -0.5),
        "c_re": nrm(ks[8], (L, SSM_GROUPS, SSM_GROUP, SSM_STATE), SSM_STATE ** -0.5),
        "c_im": nrm(ks[9], (L, SSM_GROUPS, SSM_GROUP, SSM_STATE), SSM_STATE ** -0.5),
        "d_skip": nrm(ks[10], (L, SSM_WIDTH), 1.0),
        "w_glu": nrm(ks[11], (L, SSM_WIDTH, SSM_WIDTH), SSM_WIDTH ** -0.5),
        "b_glu": nrm(ks[12], (L, SSM_WIDTH), 0.01),
        "w_proj_a": nrm(ks[13], (L, SSM_WIDTH, D_MODEL), SSM_WIDTH ** -0.5),
        "g_sgu": 1.0 + nrm(ks[14], (L, SGU_WIDTH), 0.02),
        "w_s": nrm(ks[15], (L, SGU_GROUPS, CHUNK, CHUNK), CHUNK ** -0.5),
        "b_s": 1.0 + nrm(ks[16], (L, SGU_GROUPS, CHUNK), 0.01),
        "w_proj_b": nrm(ks[17], (L, SGU_WIDTH, D_MODEL), SGU_WIDTH ** -0.5),
        "w_out": nrm(ks[18], (L, D_MODEL, D_MODEL), D_MODEL ** -0.5),
        "g_ffn": 1.0 + nrm(ks[19], (L, D_MODEL), 0.02),
        "w_up": nrm(ks[20], (L, D_MODEL, 2 * D_FF), D_MODEL ** -0.5),
        "conv_w": nrm(ks[21], (L, CONV_WIDTH, 2 * D_FF), CONV_WIDTH ** -0.5),
        "conv_b": nrm(ks[22], (L, 2 * D_FF), 0.01),
        "w_down": nrm(ks[23], (L, D_FF, D_MODEL), D_FF ** -0.5),
        "g_final": 1.0 + nrm(ks[24], (D_MODEL,), 0.02),
    }


def reference(x, g_mix, w_in, a_re, a_im, log_dt, b_re, b_im, c_re, c_im, d_skip,
              w_glu, b_glu, w_proj_a, g_sgu, w_s, b_s, w_proj_b, w_out, g_ffn,
              w_up, conv_w, conv_b, w_down, g_final):
    for l in range(DEPTH):
        h = rms_norm(x, g_mix[l])
        p = h @ w_in[l]
        u_ssm = p[..., :SSM_WIDTH]
        uv_sgu = p[..., SSM_WIDTH:SSM_WIDTH + 2 * SGU_WIDTH]
        gate_logits = p[..., SSM_WIDTH + 2 * SGU_WIDTH:]
        y_a = s5_branch(u_ssm, a_re[l], a_im[l], log_dt[l], b_re[l], b_im[l],
                        c_re[l], c_im[l], d_skip[l], w_glu[l], b_glu[l]) @ w_proj_a[l]
        y_b = sgu_branch(uv_sgu, g_sgu[l], w_s[l], b_s[l]) @ w_proj_b[l]
        g_a, g_b = jnp.split(jax.nn.sigmoid(gate_logits), N_BRANCH, axis=-1)
        x = x + (g_a * y_a + g_b * y_b) @ w_out[l]
        h = rms_norm(x, g_ffn[l])
        up = causal_depthwise_conv(h @ w_up[l], conv_w[l], conv_b[l])
        a, b = jnp.split(up, 2, axis=-1)
        x = x + (jax.nn.silu(a) * b) @ w_down[l]
    return rms_norm(x, g_final)
```

```python
import functools

import jax
import jax.numpy as jnp
from jax import lax
from jax.experimental import pallas as pl
from jax.experimental.pallas import tpu as pltpu

D_MODEL = 1024
BATCH = 8
SEQ = 4096
SSM_WIDTH = 512
SSM_GROUP = 16
SSM_GROUPS = 32
SSM_STATE = 64
STATE_COLS = SSM_GROUPS * SSM_STATE
SGU_WIDTH = 512
SGU_GROUPS = 8
SGU_GROUP_DIM = 64
CHUNK = 128
LANES = 128
SGU_LANE_TILES = SGU_WIDTH // LANES
D_FF = 2816
CONV_WIDTH = 3
EPS = 1e-6

ROWS = CHUNK * BATCH
N_STEPS = SEQ // CHUNK
SLAB = 256
SCAN_ROWS = 128
SCAN_STEPS = SCAN_ROWS // BATCH
SCAN_COLS = 1024
U_BLK = 128
ST_BLK = U_BLK // SSM_GROUP * SSM_STATE
N_BLK = SSM_WIDTH // U_BLK
FF_TILE = 256
N_FF_TILES = D_FF // FF_TILE
VMEM_LIMIT = 60 * 1024 * 1024

F32 = jnp.float32
BF16 = jnp.bfloat16


def _dot(a, b):
    return jnp.dot(a, b, preferred_element_type=F32)


def _rms(x, g):
    return x * lax.rsqrt(jnp.mean(x * x, axis=-1, keepdims=True) + EPS) * g


def _prep_kernel(are_ref, aim_ref, ldt_ref, bre_ref, bim_ref, ws_ref,
                 abre_ref, abim_ref, bbre_ref, bbim_ref, wsm_ref):
    dt = jnp.exp(ldt_ref[...])
    ar = are_ref[...]
    ai = aim_ref[...]
    mag = jnp.exp(dt * ar)
    abar_re = mag * jnp.cos(dt * ai)
    abar_im = mag * jnp.sin(dt * ai)
    den = ar * ar + ai * ai
    nr = abar_re - 1.0
    ni = abar_im
    f_re = (nr * ar + ni * ai) / den
    f_im = (ni * ar - nr * ai) / den
    abre_ref[...] = abar_re
    abim_ref[...] = abar_im
    br = bre_ref[...]
    bi = bim_ref[...]
    fr = f_re[:, None, :]
    fi = f_im[:, None, :]
    bbre_ref[...] = fr * br - fi * bi
    bbim_ref[...] = fr * bi + fi * br
    row = lax.broadcasted_iota(jnp.int32, (SGU_GROUPS, CHUNK, CHUNK), 1)
    col = lax.broadcasted_iota(jnp.int32, (SGU_GROUPS, CHUNK, CHUNK), 2)
    wsm_ref[...] = jnp.where(row >= col, ws_ref[...], 0.0)


def _prep(a_re, a_im, log_dt, b_re_t, b_im_t, w_s):
    g, p, h = SSM_GROUPS, SSM_STATE, SSM_GROUP
    return pl.pallas_call(
        _prep_kernel,
        out_shape=(
            jax.ShapeDtypeStruct((g, p), F32),
            jax.ShapeDtypeStruct((g, p), F32),
            jax.ShapeDtypeStruct((g, h, p), F32),
            jax.ShapeDtypeStruct((g, h, p), F32),
            jax.ShapeDtypeStruct((SGU_GROUPS, CHUNK, CHUNK), F32),
        ),
        name="prep",
    )(a_re, a_im, log_dt, b_re_t, b_im_t, w_s)


def _mixer_kernel(x_ref, gmix_ref, win_u_ref, win_uv_ref, win_ga_ref, win_gb_ref,
                  bblk_ref, are_ref, aim_ref, cblk_ref, dskip_ref, wglu_ref, bglu_ref,
                  wpa_ref, gsgu_ref, ws_ref, bsb_ref, wpb_ref, wout_ref,
                  o_ref,
                  h_bf, st_re, st_im, bu_re, bu_im, mix, us, v_scr, mixed_scr):
    @pl.when(pl.program_id(0) == 0)
    def _():
        st_re[...] = jnp.zeros_like(st_re)
        st_im[...] = jnp.zeros_like(st_im)

    def norm_slab(j, c):
        r = pl.multiple_of(j * SLAB, SLAB)
        h_bf[pl.ds(r, SLAB), :] = _rms(x_ref[pl.ds(r, SLAB), :], gmix_ref[...]).astype(BF16)
        return c

    lax.fori_loop(0, ROWS // SLAB, norm_slab, 0)

    def s5_chunk(j, c):
        r = pl.multiple_of(j * SCAN_ROWS, SCAN_ROWS)
        hs = h_bf[pl.ds(r, SCAN_ROWS), :]
        u = _dot(hs, win_u_ref[...])
        ub = u.astype(BF16)
        for k in range(N_BLK):
            bu = _dot(ub[:, k * U_BLK:(k + 1) * U_BLK], bblk_ref[k])
            bu_re[:, k * ST_BLK:(k + 1) * ST_BLK] = bu[:, :ST_BLK]
            bu_im[:, k * ST_BLK:(k + 1) * ST_BLK] = bu[:, ST_BLK:]
        for cb in range(STATE_COLS // SCAN_COLS):
            cs = slice(cb * SCAN_COLS, (cb + 1) * SCAN_COLS)
            ar = are_ref[:, cs]
            ai = aim_ref[:, cs]

            def step(t, carry, cs=cs, ar=ar, ai=ai):
                sre, sim = carry
                rr = pl.multiple_of(t * BATCH, BATCH)
                nre = ar * sre - ai * sim + bu_re[pl.ds(rr, BATCH), cs]
                nim = ar * sim + ai * sre + bu_im[pl.ds(rr, BATCH), cs]
                bu_re[pl.ds(rr, BATCH), cs] = nre
                bu_im[pl.ds(rr, BATCH), cs] = nim
                return nre, nim

            sre, sim = lax.fori_loop(0, SCAN_STEPS, step, (st_re[:, cs], st_im[:, cs]))
            st_re[:, cs] = sre
            st_im[:, cs] = sim
        ys = []
        for k in range(N_BLK):
            sr = bu_re[:, k * ST_BLK:(k + 1) * ST_BLK].astype(BF16)
            si = bu_im[:, k * ST_BLK:(k + 1) * ST_BLK].astype(BF16)
            ys.append(_dot(sr, cblk_ref[k, :ST_BLK, :]) + _dot(si, cblk_ref[k, ST_BLK:, :]))
        y = jnp.concatenate(ys, axis=1) + dskip_ref[...] * u
        y = jax.nn.gelu(y)
        y = y * jax.nn.sigmoid(_dot(y.astype(BF16), wglu_ref[...]) + bglu_ref[...])
        ya = _dot(y.astype(BF16), wpa_ref[...])
        ga = jax.nn.sigmoid(_dot(hs, win_ga_ref[...]))
        mix[pl.ds(r, SCAN_ROWS), :] = ga * ya
        return c

    lax.fori_loop(0, ROWS // SCAN_ROWS, s5_chunk, 0)

    def sgu_in(j, c):
        r = pl.multiple_of(j * SLAB, SLAB)
        uv = jax.nn.gelu(_dot(h_bf[pl.ds(r, SLAB), :], win_uv_ref[...]))
        us[pl.ds(r, SLAB), :] = uv[:, :SGU_WIDTH]
        v = _rms(uv[:, SGU_WIDTH:], gsgu_ref[...])
        for k in range(SGU_LANE_TILES):
            v_scr[k, pl.ds(r, SLAB), :] = v[:, k * LANES:(k + 1) * LANES]
        return c

    lax.fori_loop(0, ROWS // SLAB, sgu_in, 0)

    for b in range(BATCH):
        for k in range(SGU_LANE_TILES):
            vb = v_scr[k, pl.ds(b, CHUNK, stride=BATCH), :].astype(BF16)
            outs = [
                _dot(ws_ref[2 * k + e], vb[:, e * SGU_GROUP_DIM:(e + 1) * SGU_GROUP_DIM])
                for e in range(2)
            ]
            mixed_scr[k, pl.ds(b, CHUNK, stride=BATCH), :] = (
                jnp.concatenate(outs, axis=1) + bsb_ref[:, k * LANES:(k + 1) * LANES])

    def merge(j, c):
        r = pl.multiple_of(j * SLAB, SLAB)
        rows = pl.ds(r, SLAB)
        mixed = jnp.concatenate([mixed_scr[k, rows, :] for k in range(SGU_LANE_TILES)], axis=1)
        z = (us[rows, :] * mixed).astype(BF16)
        yb = _dot(z, wpb_ref[...])
        gb = jax.nn.sigmoid(_dot(h_bf[rows, :], win_gb_ref[...]))
        m = mix[rows, :] + gb * yb
        o_ref[rows, :] = x_ref[rows, :] + _dot(m.astype(BF16), wout_ref[...])
        return c

    lax.fori_loop(0, ROWS // SLAB, merge, 0)


def _const_spec(shape):
    nd = len(shape)
    return pl.BlockSpec(shape, lambda i, nd=nd: (0,) * nd, pipeline_mode=pl.Buffered(1))


def _mixer(xt, consts):
    n = xt.shape[0]
    row_spec = pl.BlockSpec((ROWS, D_MODEL), lambda i: (i, 0))
    return pl.pallas_call(
        _mixer_kernel,
        grid=(N_STEPS,),
        in_specs=[row_spec] + [_const_spec(c.shape) for c in consts],
        out_specs=row_spec,
        out_shape=jax.ShapeDtypeStruct((n, D_MODEL), F32),
        scratch_shapes=[
            pltpu.VMEM((ROWS, D_MODEL), BF16),
            pltpu.VMEM((BATCH, STATE_COLS), F32),
            pltpu.VMEM((BATCH, STATE_COLS), F32),
            pltpu.VMEM((SCAN_ROWS, STATE_COLS), F32),
            pltpu.VMEM((SCAN_ROWS, STATE_COLS), F32),
            pltpu.VMEM((ROWS, D_MODEL), F32),
            pltpu.VMEM((ROWS, SGU_WIDTH), F32),
            pltpu.VMEM((SGU_LANE_TILES, ROWS, LANES), F32),
            pltpu.VMEM((SGU_LANE_TILES, ROWS, LANES), F32),
        ],
        compiler_params=pltpu.CompilerParams(
            dimension_semantics=("arbitrary",), vmem_limit_bytes=VMEM_LIMIT),
        name="mixer",
    )(xt, *consts)


def _ffn_kernel(x_ref, gffn_ref, wua_ref, wub_ref, cwa_ref, cwb_ref, cba_ref, cbb_ref,
                wd_ref, gfin_ref, o_ref, h_bf, acc, carry_a, carry_b):
    @pl.when(pl.program_id(0) == 0)
    def _():
        carry_a[...] = jnp.zeros_like(carry_a)
        carry_b[...] = jnp.zeros_like(carry_b)

    def norm_slab(j, c):
        r = pl.multiple_of(j * SLAB, SLAB)
        h_bf[pl.ds(r, SLAB), :] = _rms(x_ref[pl.ds(r, SLAB), :], gffn_ref[...]).astype(BF16)
        return c

    lax.fori_loop(0, ROWS // SLAB, norm_slab, 0)
    acc[...] = jnp.zeros_like(acc)

    def conv(y, carry_ref, c, cw_ref, cb_ref):
        prev = carry_ref[c]
        carry_ref[c] = y[ROWS - 2 * BATCH:, :]
        y1 = jnp.concatenate([prev[BATCH:, :], y[:ROWS - BATCH, :]], axis=0)
        y2 = jnp.concatenate([prev, y[:ROWS - 2 * BATCH, :]], axis=0)
        w = cw_ref[c]
        return y2 * w[0:1, :] + y1 * w[1:2, :] + y * w[2:3, :] + cb_ref[c]

    def ff_tile(c, carry):
        hb = h_bf[...]
        a = conv(_dot(hb, wua_ref[c]), carry_a, c, cwa_ref, cba_ref)
        b = conv(_dot(hb, wub_ref[c]), carry_b, c, cwb_ref, cbb_ref)
        gated = (jax.nn.silu(a) * b).astype(BF16)
        acc[...] += _dot(gated, wd_ref[c])
        return carry

    lax.fori_loop(0, N_FF_TILES, ff_tile, 0)

    def out_slab(j, c):
        r = pl.multiple_of(j * SLAB, SLAB)
        rows = pl.ds(r, SLAB)
        o_ref[rows, :] = _rms(x_ref[rows, :] + acc[rows, :], gfin_ref[...])
        return c

    lax.fori_loop(0, ROWS // SLAB, out_slab, 0)


def _ffn(x1, consts):
    n = x1.shape[0]
    row_spec = pl.BlockSpec((ROWS, D_MODEL), lambda i: (i, 0))
    return pl.pallas_call(
        _ffn_kernel,
        grid=(N_STEPS,),
        in_specs=[row_spec] + [_const_spec(c.shape) for c in consts],
        out_specs=row_spec,
        out_shape=jax.ShapeDtypeStruct((n, D_MODEL), F32),
        scratch_shapes=[
            pltpu.VMEM((ROWS, D_MODEL), BF16),
            pltpu.VMEM((ROWS, D_MODEL), F32),
            pltpu.VMEM((N_FF_TILES, 2 * BATCH, FF_TILE), F32),
            pltpu.VMEM((N_FF_TILES, 2 * BATCH, FF_TILE), F32),
        ],
        compiler_params=pltpu.CompilerParams(
            dimension_semantics=("arbitrary",), vmem_limit_bytes=VMEM_LIMIT),
        name="ffn",
    )(x1, *consts)


def _block_diag(blocks):
    g, r, c = blocks.shape
    eye = jnp.eye(g, dtype=blocks.dtype)
    return jnp.einsum("grc,gk->grkc", blocks, eye).reshape(g * r, g * c)


def kernel(x, g_mix, w_in, a_re, a_im, log_dt, b_re, b_im, c_re, c_im, d_skip, w_glu, b_glu, w_proj_a, g_sgu, w_s, b_s, w_proj_b, w_out, g_ffn, w_up, conv_w, conv_b, w_down, g_final):
    l = 0
    abar_re, abar_im, bbar_re, bbar_im, ws_m = _prep(
        a_re[l], a_im[l], log_dt[l][:, None],
        b_re[l].transpose(0, 2, 1), b_im[l].transpose(0, 2, 1), w_s[l])
    are = jnp.broadcast_to(abar_re.reshape(1, STATE_COLS), (BATCH, STATE_COLS))
    aim = jnp.broadcast_to(abar_im.reshape(1, STATE_COLS), (BATCH, STATE_COLS))
    bd_re = _block_diag(bbar_re)
    bd_im = _block_diag(bbar_im)
    bblk = jnp.stack([
        jnp.concatenate([bd_re[k * U_BLK:(k + 1) * U_BLK, k * ST_BLK:(k + 1) * ST_BLK],
                         bd_im[k * U_BLK:(k + 1) * U_BLK, k * ST_BLK:(k + 1) * ST_BLK]], axis=1)
        for k in range(N_BLK)]).astype(BF16)
    cd_re = _block_diag(c_re[l].transpose(0, 2, 1))
    cd_im = _block_diag(c_im[l].transpose(0, 2, 1))
    cblk = jnp.stack([
        jnp.concatenate([cd_re[k * ST_BLK:(k + 1) * ST_BLK, k * U_BLK:(k + 1) * U_BLK],
                         -cd_im[k * ST_BLK:(k + 1) * ST_BLK, k * U_BLK:(k + 1) * U_BLK]], axis=0)
        for k in range(N_BLK)]).astype(BF16)
    win = w_in[l].astype(BF16)
    o1 = SSM_WIDTH
    o2 = o1 + 2 * SGU_WIDTH
    o3 = o2 + D_MODEL
    bsb = jnp.repeat(b_s[l].T, SGU_GROUP_DIM, axis=1)
    mixer_consts = [
        g_mix[l][None, :], win[:, :o1], win[:, o1:o2], win[:, o2:o3], win[:, o3:],
        bblk, are, aim, cblk, d_skip[l][None, :], w_glu[l].astype(BF16), b_glu[l][None, :],
        w_proj_a[l].astype(BF16), g_sgu[l][None, :], ws_m.astype(BF16), bsb,
        w_proj_b[l].astype(BF16), w_out[l].astype(BF16),
    ]
    wup = w_up[l].astype(BF16)
    tile = lambda w: w.reshape(w.shape[0], N_FF_TILES, FF_TILE).transpose(1, 0, 2)
    ffn_consts = [
        g_ffn[l][None, :],
        tile(wup[:, :D_FF]), tile(wup[:, D_FF:]),
        tile(conv_w[l][:, :D_FF]), tile(conv_w[l][:, D_FF:]),
        tile(conv_b[l][None, :D_FF]), tile(conv_b[l][None, D_FF:]),
        w_down[l].astype(BF16).reshape(N_FF_TILES, FF_TILE, D_MODEL),
        g_final[None, :],
    ]

    xt = x.transpose(1, 0, 2).reshape(SEQ * BATCH, D_MODEL)
    x1 = _mixer(xt, mixer_consts)
    out = _ffn(x1, ffn_consts)
    return out.reshape(SEQ, BATCH, D_MODEL).transpose(1, 0, 2)
```

```python
import jax
import jax.numpy as jnp
from jax import lax
from jax.experimental import pallas as pl
from jax.experimental.pallas import tpu as pltpu

D_MODEL = 1024
BATCH = 8
SEQ = 4096
SSM_WIDTH = 512
SSM_GROUP = 16
SSM_GROUPS = 32
SSM_STATE = 64
STATE_COLS = SSM_GROUPS * SSM_STATE
SGU_WIDTH = 512
SGU_GROUPS = 8
SGU_GROUP_DIM = 64
CHUNK = 128
LANES = 128
D_TILES = D_MODEL // LANES
SSM_TILES = SSM_WIDTH // LANES
SGU_TILES = SGU_WIDTH // LANES
D_FF = 2816
CONV_WIDTH = 3
EPS = 1e-6

ROWS = CHUNK * BATCH
N_STEPS = SEQ // CHUNK
SLAB = 256
SLAB_BATCHES = SLAB // CHUNK
SCAN_STEPS = 16
SCAN_ROWS = SCAN_STEPS * BATCH
SCAN_COLS = 1024
U_BLK = 128
ST_BLK = U_BLK // SSM_GROUP * SSM_STATE
N_BLK = SSM_WIDTH // U_BLK
FF_TILE = 256
N_FF_TILES = D_FF // FF_TILE
VMEM_LIMIT = 60 * 1024 * 1024

F32 = jnp.float32
BF16 = jnp.bfloat16


def _dot(a, b):
    return jnp.dot(a, b, preferred_element_type=F32)


def _rms(x, g):
    return x * lax.rsqrt(jnp.mean(x * x, axis=-1, keepdims=True) + EPS) * g


def _lane_tiles(ref, rows, n):
    return jnp.concatenate([ref[k, rows, :] for k in range(n)], axis=1)


def _prep_kernel(are_ref, aim_ref, ldt_ref, bre_ref, bim_ref, ws_ref,
                 abre_ref, abim_ref, bbre_ref, bbim_ref, wsm_ref):
    dt = jnp.exp(ldt_ref[...])
    ar = are_ref[...]
    ai = aim_ref[...]
    mag = jnp.exp(dt * ar)
    abar_re = mag * jnp.cos(dt * ai)
    abar_im = mag * jnp.sin(dt * ai)
    den = ar * ar + ai * ai
    nr = abar_re - 1.0
    ni = abar_im
    f_re = (nr * ar + ni * ai) / den
    f_im = (ni * ar - nr * ai) / den
    abre_ref[...] = abar_re
    abim_ref[...] = abar_im
    br = bre_ref[...]
    bi = bim_ref[...]
    fr = f_re[:, None, :]
    fi = f_im[:, None, :]
    bbre_ref[...] = fr * br - fi * bi
    bbim_ref[...] = fr * bi + fi * br
    row = lax.broadcasted_iota(jnp.int32, (SGU_GROUPS, CHUNK, CHUNK), 1)
    col = lax.broadcasted_iota(jnp.int32, (SGU_GROUPS, CHUNK, CHUNK), 2)
    wsm_ref[...] = jnp.where(row >= col, ws_ref[...], 0.0)


def _prep(a_re, a_im, log_dt, b_re_t, b_im_t, w_s):
    g, p, h = SSM_GROUPS, SSM_STATE, SSM_GROUP
    return pl.pallas_call(
        _prep_kernel,
        out_shape=(
            jax.ShapeDtypeStruct((g, p), F32),
            jax.ShapeDtypeStruct((g, p), F32),
            jax.ShapeDtypeStruct((g, h, p), F32),
            jax.ShapeDtypeStruct((g, h, p), F32),
            jax.ShapeDtypeStruct((SGU_GROUPS, CHUNK, CHUNK), F32),
        ),
        name="prep",
    )(a_re, a_im, log_dt, b_re_t, b_im_t, w_s)


def _mixer_kernel(x_ref, gmix_ref, win_u_ref, win_uv_ref, win_ga_ref, win_gb_ref,
                  bblk_ref, are_ref, aim_ref, cblk_ref, dskip_ref, wglu_ref, bglu_ref,
                  wpa_ref, gsgu_ref, wsp_ref, bsb_ref, wpb_ref, wout_ref,
                  o_ref,
                  h_bf, st_re, st_im, bu_re, bu_im, u_scr, ya_scr):
    @pl.when(pl.program_id(0) == 0)
    def _():
        st_re[...] = jnp.zeros_like(st_re)
        st_im[...] = jnp.zeros_like(st_im)

    def x_slab(j):
        return x_ref[pl.ds(j * SLAB_BATCHES, SLAB_BATCHES)].reshape(SLAB, D_MODEL)

    def norm_slab(j, c):
        rows = pl.ds(pl.multiple_of(j * SLAB, SLAB), SLAB)
        h = _rms(x_slab(j), gmix_ref[...]).astype(BF16)
        h_bf[rows, :] = h
        u = _dot(h, win_u_ref[...])
        for k in range(SSM_TILES):
            u_scr[k, rows, :] = u[:, k * LANES:(k + 1) * LANES]
        return c

    lax.fori_loop(0, ROWS // SLAB, norm_slab, 0)

    def s5_chunk(j, c):
        t0 = j * SCAN_STEPS
        u = jnp.concatenate([
            jnp.concatenate([u_scr[k, pl.ds(t0 + t, BATCH, stride=CHUNK), :]
                             for k in range(SSM_TILES)], axis=1)
            for t in range(SCAN_STEPS)], axis=0)
        ub = u.astype(BF16)
        for k in range(N_BLK):
            bu = _dot(ub[:, k * U_BLK:(k + 1) * U_BLK], bblk_ref[k])
            bu_re[:, k * ST_BLK:(k + 1) * ST_BLK] = bu[:, :ST_BLK]
            bu_im[:, k * ST_BLK:(k + 1) * ST_BLK] = bu[:, ST_BLK:]
        for cb in range(STATE_COLS // SCAN_COLS):
            cs = slice(cb * SCAN_COLS, (cb + 1) * SCAN_COLS)
            ar = are_ref[:, cs]
            ai = aim_ref[:, cs]

            def step(t, carry, cs=cs, ar=ar, ai=ai):
                sre, sim = carry
                rr = pl.ds(pl.multiple_of(t * BATCH, BATCH), BATCH)
                nre = ar * sre - ai * sim + bu_re[rr, cs]
                nim = ar * sim + ai * sre + bu_im[rr, cs]
                bu_re[rr, cs] = nre
                bu_im[rr, cs] = nim
                return nre, nim

            sre, sim = lax.fori_loop(0, SCAN_STEPS, step, (st_re[:, cs], st_im[:, cs]))
            st_re[:, cs] = sre
            st_im[:, cs] = sim
        ys = []
        for k in range(N_BLK):
            sr = bu_re[:, k * ST_BLK:(k + 1) * ST_BLK].astype(BF16)
            si = bu_im[:, k * ST_BLK:(k + 1) * ST_BLK].astype(BF16)
            ys.append(_dot(sr, cblk_ref[k, :ST_BLK, :]) + _dot(si, cblk_ref[k, ST_BLK:, :]))
        y = jnp.concatenate(ys, axis=1) + dskip_ref[...] * u
        y = jax.nn.gelu(y)
        y = y * jax.nn.sigmoid(_dot(y.astype(BF16), wglu_ref[...]) + bglu_ref[...])
        for t in range(SCAN_STEPS):
            for k in range(SSM_TILES):
                ya_scr[k, pl.ds(t0 + t, BATCH, stride=CHUNK), :] = (
                    y[t * BATCH:(t + 1) * BATCH, k * LANES:(k + 1) * LANES])
        return c

    lax.fori_loop(0, CHUNK // SCAN_STEPS, s5_chunk, 0)

    lane = lax.broadcasted_iota(jnp.int32, (CHUNK, LANES), 1)
    first_group = lane < SGU_GROUP_DIM

    def merge(j, c):
        rows = pl.ds(pl.multiple_of(j * SLAB, SLAB), SLAB)
        hs = h_bf[rows, :]
        uv = jax.nn.gelu(_dot(hs, win_uv_ref[...]))
        v = _rms(uv[:, SGU_WIDTH:], gsgu_ref[...])
        mixed = []
        for e in range(SLAB_BATCHES):
            tiles = []
            for k in range(SGU_TILES):
                vt = v[e * CHUNK:(e + 1) * CHUNK, k * LANES:(k + 1) * LANES]
                rhs = jnp.concatenate([jnp.where(first_group, vt, 0.0).astype(BF16),
                                       jnp.where(first_group, 0.0, vt).astype(BF16)], axis=0)
                tiles.append(_dot(wsp_ref[k], rhs))
            mixed.append(jnp.concatenate(tiles, axis=1) + bsb_ref[...])
        mixed = jnp.concatenate(mixed, axis=0)
        yb = _dot((uv[:, :SGU_WIDTH] * mixed).astype(BF16), wpb_ref[...])
        gb = jax.nn.sigmoid(_dot(hs, win_gb_ref[...]))
        ya = _dot(_lane_tiles(ya_scr, rows, SSM_TILES).astype(BF16), wpa_ref[...])
        ga = jax.nn.sigmoid(_dot(hs, win_ga_ref[...]))
        m = ga * ya + gb * yb
        x1 = x_slab(j) + _dot(m.astype(BF16), wout_ref[...])
        for e in range(SLAB_BATCHES):
            for k in range(D_TILES):
                o_ref[k, pl.ds(j * SLAB_BATCHES + e, CHUNK, stride=BATCH), :] = (
                    x1[e * CHUNK:(e + 1) * CHUNK, k * LANES:(k + 1) * LANES])
        return c

    lax.fori_loop(0, ROWS // SLAB, merge, 0)


def _const_spec(shape):
    nd = len(shape)
    return pl.BlockSpec(shape, lambda i, nd=nd: (0,) * nd, pipeline_mode=pl.Buffered(1))


def _mixer(x, consts):
    return pl.pallas_call(
        _mixer_kernel,
        grid=(N_STEPS,),
        in_specs=[pl.BlockSpec((BATCH, CHUNK, D_MODEL), lambda i: (0, i, 0))]
        + [_const_spec(c.shape) for c in consts],
        out_specs=pl.BlockSpec((D_TILES, ROWS, LANES), lambda i: (0, i, 0)),
        out_shape=jax.ShapeDtypeStruct((D_TILES, SEQ * BATCH, LANES), F32),
        scratch_shapes=[
            pltpu.VMEM((ROWS, D_MODEL), BF16),
            pltpu.VMEM((BATCH, STATE_COLS), F32),
            pltpu.VMEM((BATCH, STATE_COLS), F32),
            pltpu.VMEM((SCAN_ROWS, STATE_COLS), F32),
            pltpu.VMEM((SCAN_ROWS, STATE_COLS), F32),
            pltpu.VMEM((SSM_TILES, ROWS, LANES), F32),
            pltpu.VMEM((SSM_TILES, ROWS, LANES), F32),
        ],
        compiler_params=pltpu.CompilerParams(
            dimension_semantics=("arbitrary",), vmem_limit_bytes=VMEM_LIMIT),
        name="mixer",
    )(x, *consts)


def _ffn_kernel(x_ref, gffn_ref, wua_ref, wub_ref, cwa_ref, cwb_ref, cba_ref, cbb_ref,
                wd_ref, gfin_ref, o_ref, h_bf, acc, carry_a, carry_b, fin_scr):
    @pl.when(pl.program_id(0) == 0)
    def _():
        carry_a[...] = jnp.zeros_like(carry_a)
        carry_b[...] = jnp.zeros_like(carry_b)

    def norm_slab(j, c):
        rows = pl.ds(pl.multiple_of(j * SLAB, SLAB), SLAB)
        h_bf[rows, :] = _rms(_lane_tiles(x_ref, rows, D_TILES), gffn_ref[...]).astype(BF16)
        return c

    lax.fori_loop(0, ROWS // SLAB, norm_slab, 0)
    acc[...] = jnp.zeros_like(acc)

    def conv(y, carry_ref, c, cw_ref, cb_ref):
        prev = carry_ref[c]
        carry_ref[c] = y[ROWS - 2 * BATCH:, :]
        y1 = jnp.concatenate([prev[BATCH:, :], y[:ROWS - BATCH, :]], axis=0)
        y2 = jnp.concatenate([prev, y[:ROWS - 2 * BATCH, :]], axis=0)
        w = cw_ref[c]
        return y2 * w[0:1, :] + y1 * w[1:2, :] + y * w[2:3, :] + cb_ref[c]

    def ff_tile(c, carry):
        hb = h_bf[...]
        a = conv(_dot(hb, wua_ref[c]), carry_a, c, cwa_ref, cba_ref)
        b = conv(_dot(hb, wub_ref[c]), carry_b, c, cwb_ref, cbb_ref)
        gated = (jax.nn.silu(a) * b).astype(BF16)
        acc[...] += _dot(gated, wd_ref[c])
        return carry

    lax.fori_loop(0, N_FF_TILES, ff_tile, 0)

    def out_slab(j, c):
        rows = pl.ds(pl.multiple_of(j * SLAB, SLAB), SLAB)
        y = _rms(_lane_tiles(x_ref, rows, D_TILES) + acc[rows, :], gfin_ref[...])
        for k in range(D_TILES):
            fin_scr[k, rows, :] = y[:, k * LANES:(k + 1) * LANES]
        return c

    lax.fori_loop(0, ROWS // SLAB, out_slab, 0)

    for b in range(BATCH):
        for k in range(D_TILES):
            o_ref[b, :, k * LANES:(k + 1) * LANES] = fin_scr[k, pl.ds(b, CHUNK, stride=BATCH), :]


def _ffn(x1, consts):
    return pl.pallas_call(
        _ffn_kernel,
        grid=(N_STEPS,),
        in_specs=[pl.BlockSpec((D_TILES, ROWS, LANES), lambda i: (0, i, 0))]
        + [_const_spec(c.shape) for c in consts],
        out_specs=pl.BlockSpec((BATCH, CHUNK, D_MODEL), lambda i: (0, i, 0)),
        out_shape=jax.ShapeDtypeStruct((BATCH, SEQ, D_MODEL), F32),
        scratch_shapes=[
            pltpu.VMEM((ROWS, D_MODEL), BF16),
            pltpu.VMEM((ROWS, D_MODEL), F32),
            pltpu.VMEM((N_FF_TILES, 2 * BATCH, FF_TILE), F32),
            pltpu.VMEM((N_FF_TILES, 2 * BATCH, FF_TILE), F32),
            pltpu.VMEM((D_TILES, ROWS, LANES), F32),
        ],
        compiler_params=pltpu.CompilerParams(
            dimension_semantics=("arbitrary",), vmem_limit_bytes=VMEM_LIMIT),
        name="ffn",
    )(x1, *consts)


def _block_diag(blocks):
    g, r, c = blocks.shape
    eye = jnp.eye(g, dtype=blocks.dtype)
    return jnp.einsum("grc,gk->grkc", blocks, eye).reshape(g * r, g * c)


def kernel(x, g_mix, w_in, a_re, a_im, log_dt, b_re, b_im, c_re, c_im, d_skip, w_glu, b_glu, w_proj_a, g_sgu, w_s, b_s, w_proj_b, w_out, g_ffn, w_up, conv_w, conv_b, w_down, g_final):
    l = 0
    abar_re, abar_im, bbar_re, bbar_im, ws_m = _prep(
        a_re[l], a_im[l], log_dt[l][:, None],
        b_re[l].transpose(0, 2, 1), b_im[l].transpose(0, 2, 1), w_s[l])
    are = jnp.broadcast_to(abar_re.reshape(1, STATE_COLS), (BATCH, STATE_COLS))
    aim = jnp.broadcast_to(abar_im.reshape(1, STATE_COLS), (BATCH, STATE_COLS))
    bd_re = _block_diag(bbar_re)
    bd_im = _block_diag(bbar_im)
    bblk = jnp.stack([
        jnp.concatenate([bd_re[k * U_BLK:(k + 1) * U_BLK, k * ST_BLK:(k + 1) * ST_BLK],
                         bd_im[k * U_BLK:(k + 1) * U_BLK, k * ST_BLK:(k + 1) * ST_BLK]], axis=1)
        for k in range(N_BLK)]).astype(BF16)
    cd_re = _block_diag(c_re[l].transpose(0, 2, 1))
    cd_im = _block_diag(c_im[l].transpose(0, 2, 1))
    cblk = jnp.stack([
        jnp.concatenate([cd_re[k * ST_BLK:(k + 1) * ST_BLK, k * U_BLK:(k + 1) * U_BLK],
                         -cd_im[k * ST_BLK:(k + 1) * ST_BLK, k * U_BLK:(k + 1) * U_BLK]], axis=0)
        for k in range(N_BLK)]).astype(BF16)
    win = w_in[l].astype(BF16)
    o1 = SSM_WIDTH
    o2 = o1 + 2 * SGU_WIDTH
    o3 = o2 + D_MODEL
    wsp = ws_m.reshape(SGU_TILES, 2, CHUNK, CHUNK).transpose(0, 2, 1, 3).reshape(
        SGU_TILES, CHUNK, 2 * CHUNK).astype(BF16)
    bsb = jnp.repeat(b_s[l].T, SGU_GROUP_DIM, axis=1)
    mixer_consts = [
        g_mix[l][None, :], win[:, :o1], win[:, o1:o2], win[:, o2:o3], win[:, o3:],
        bblk, are, aim, cblk, d_skip[l][None, :], w_glu[l].astype(BF16), b_glu[l][None, :],
        w_proj_a[l].astype(BF16), g_sgu[l][None, :], wsp, bsb,
        w_proj_b[l].astype(BF16), w_out[l].astype(BF16),
    ]
    wup = w_up[l].astype(BF16)
    tile = lambda w: w.reshape(w.shape[0], N_FF_TILES, FF_TILE).transpose(1, 0, 2)
    ffn_consts = [
        g_ffn[l][None, :],
        tile(wup[:, :D_FF]), tile(wup[:, D_FF:]),
        tile(conv_w[l][:, :D_FF]), tile(conv_w[l][:, D_FF:]),
        tile(conv_b[l][None, :D_FF]), tile(conv_b[l][None, D_FF:]),
        w_down[l].astype(BF16).reshape(N_FF_TILES, FF_TILE, D_MODEL),
        g_final[None, :],
    ]
    x1 = _mixer(x, mixer_consts)
    return _ffn(x1, ffn_consts)
```

```python
import jax
import jax.numpy as jnp
from jax import lax
from jax.experimental import pallas as pl
from jax.experimental.pallas import tpu as pltpu

D_MODEL = 1024
BATCH = 8
SEQ = 4096
SSM_WIDTH = 512
SSM_GROUP = 16
SSM_GROUPS = 32
SSM_STATE = 64
STATE_COLS = SSM_GROUPS * SSM_STATE
SGU_WIDTH = 512
SGU_GROUPS = 8
SGU_GROUP_DIM = 64
CHUNK = 128
LANES = 128
D_TILES = D_MODEL // LANES
SSM_TILES = SSM_WIDTH // LANES
SGU_TILES = SGU_WIDTH // LANES
D_FF = 2816
CONV_WIDTH = 3
EPS = 1e-6

ROWS = CHUNK * BATCH
N_STEPS = SEQ // CHUNK
SLAB = 256
SLAB_BATCHES = SLAB // CHUNK
SCAN_STEPS = 16
SCAN_ROWS = SCAN_STEPS * BATCH
SCAN_COLS = 1024
U_BLK = 128
ST_BLK = U_BLK // SSM_GROUP * SSM_STATE
N_BLK = SSM_WIDTH // U_BLK
FF_TILE = 256
N_FF_TILES = D_FF // FF_TILE
VMEM_LIMIT = 60 * 1024 * 1024

F32 = jnp.float32
BF16 = jnp.bfloat16


def _dot(a, b):
    return jnp.dot(a, b, preferred_element_type=F32)


def _rms(x, g):
    return x * lax.rsqrt(jnp.mean(x * x, axis=-1, keepdims=True) + EPS) * g


def _lane_tiles(ref, rows, n):
    return jnp.concatenate([ref[k, rows, :] for k in range(n)], axis=1)


def _prep_kernel(are_ref, aim_ref, ldt_ref, bre_ref, bim_ref, ws_ref,
                 abre_ref, abim_ref, bbre_ref, bbim_ref, wsm_ref):
    dt = jnp.exp(ldt_ref[...])
    ar = are_ref[...]
    ai = aim_ref[...]
    mag = jnp.exp(dt * ar)
    abar_re = mag * jnp.cos(dt * ai)
    abar_im = mag * jnp.sin(dt * ai)
    den = ar * ar + ai * ai
    nr = abar_re - 1.0
    ni = abar_im
    f_re = (nr * ar + ni * ai) / den
    f_im = (ni * ar - nr * ai) / den
    abre_ref[...] = abar_re
    abim_ref[...] = abar_im
    br = bre_ref[...]
    bi = bim_ref[...]
    fr = f_re[:, None, :]
    fi = f_im[:, None, :]
    bbre_ref[...] = fr * br - fi * bi
    bbim_ref[...] = fr * bi + fi * br
    row = lax.broadcasted_iota(jnp.int32, (SGU_GROUPS, CHUNK, CHUNK), 1)
    col = lax.broadcasted_iota(jnp.int32, (SGU_GROUPS, CHUNK, CHUNK), 2)
    wsm_ref[...] = jnp.where(row >= col, ws_ref[...], 0.0)


def _prep(a_re, a_im, log_dt, b_re_t, b_im_t, w_s):
    g, p, h = SSM_GROUPS, SSM_STATE, SSM_GROUP
    return pl.pallas_call(
        _prep_kernel,
        out_shape=(
            jax.ShapeDtypeStruct((g, p), F32),
            jax.ShapeDtypeStruct((g, p), F32),
            jax.ShapeDtypeStruct((g, h, p), F32),
            jax.ShapeDtypeStruct((g, h, p), F32),
            jax.ShapeDtypeStruct((SGU_GROUPS, CHUNK, CHUNK), F32),
        ),
        name="prep",
    )(a_re, a_im, log_dt, b_re_t, b_im_t, w_s)


def _mixer_kernel(x_ref, gmix_ref, win_u_ref, win_uv_ref, win_ga_ref, win_gb_ref,
                  bblk_ref, are_ref, aim_ref, cblk_ref, dskip_ref, wglu_ref, bglu_ref,
                  wpa_ref, gsgu_ref, wsp_ref, bsb_ref, wpb_ref, wout_ref,
                  o_ref,
                  h_bf, st_re, st_im, bu_re, bu_im, u_scr, ya_scr):
    @pl.when(pl.program_id(0) == 0)
    def _():
        st_re[...] = jnp.zeros_like(st_re)
        st_im[...] = jnp.zeros_like(st_im)

    def x_slab(j):
        return x_ref[pl.ds(j * SLAB_BATCHES, SLAB_BATCHES)].reshape(SLAB, D_MODEL)

    def norm_slab(j, c):
        rows = pl.ds(pl.multiple_of(j * SLAB, SLAB), SLAB)
        h = _rms(x_slab(j), gmix_ref[...]).astype(BF16)
        h_bf[rows, :] = h
        u = _dot(h, win_u_ref[...])
        for k in range(SSM_TILES):
            u_scr[k, rows, :] = u[:, k * LANES:(k + 1) * LANES]
        return c

    lax.fori_loop(0, ROWS // SLAB, norm_slab, 0)

    def s5_chunk(j, c):
        t0 = j * SCAN_STEPS
        u = jnp.concatenate([
            jnp.concatenate([u_scr[k, pl.ds(t0 + t, BATCH, stride=CHUNK), :]
                             for k in range(SSM_TILES)], axis=1)
            for t in range(SCAN_STEPS)], axis=0)
        ub = u.astype(BF16)
        for k in range(N_BLK):
            bu = _dot(ub[:, k * U_BLK:(k + 1) * U_BLK], bblk_ref[k])
            bu_re[:, k * ST_BLK:(k + 1) * ST_BLK] = bu[:, :ST_BLK]
            bu_im[:, k * ST_BLK:(k + 1) * ST_BLK] = bu[:, ST_BLK:]
        for cb in range(STATE_COLS // SCAN_COLS):
            cs = slice(cb * SCAN_COLS, (cb + 1) * SCAN_COLS)
            ar = are_ref[:, cs]
            ai = aim_ref[:, cs]

            def step(t, carry, cs=cs, ar=ar, ai=ai):
                sre, sim = carry
                rr = pl.ds(pl.multiple_of(t * BATCH, BATCH), BATCH)
                nre = ar * sre - ai * sim + bu_re[rr, cs]
                nim = ar * sim + ai * sre + bu_im[rr, cs]
                bu_re[rr, cs] = nre
                bu_im[rr, cs] = nim
                return nre, nim

            sre, sim = lax.fori_loop(0, SCAN_STEPS, step, (st_re[:, cs], st_im[:, cs]))
            st_re[:, cs] = sre
            st_im[:, cs] = sim
        ys = []
        for k in range(N_BLK):
            sr = bu_re[:, k * ST_BLK:(k + 1) * ST_BLK].astype(BF16)
            si = bu_im[:, k * ST_BLK:(k + 1) * ST_BLK].astype(BF16)
            ys.append(_dot(sr, cblk_ref[k, :ST_BLK, :]) + _dot(si, cblk_ref[k, ST_BLK:, :]))
        y = jnp.concatenate(ys, axis=1) + dskip_ref[...] * u
        y = jax.nn.gelu(y)
        y = y * jax.nn.sigmoid(_dot(y.astype(BF16), wglu_ref[...]) + bglu_ref[...])
        for t in range(SCAN_STEPS):
            for k in range(SSM_TILES):
                ya_scr[k, pl.ds(t0 + t, BATCH, stride=CHUNK), :] = (
                    y[t * BATCH:(t + 1) * BATCH, k * LANES:(k + 1) * LANES])
        return c

    lax.fori_loop(0, CHUNK // SCAN_STEPS, s5_chunk, 0)

    lane = lax.broadcasted_iota(jnp.int32, (CHUNK, LANES), 1)
    first_group = lane < SGU_GROUP_DIM

    def merge(j, c):
        rows = pl.ds(pl.multiple_of(j * SLAB, SLAB), SLAB)
        hs = h_bf[rows, :]
        uv = jax.nn.gelu(_dot(hs, win_uv_ref[...]))
        v = _rms(uv[:, SGU_WIDTH:], gsgu_ref[...])
        mixed = []
        for e in range(SLAB_BATCHES):
            tiles = []
            for k in range(SGU_TILES):
                vt = v[e * CHUNK:(e + 1) * CHUNK, k * LANES:(k + 1) * LANES]
                rhs = jnp.concatenate([jnp.where(first_group, vt, 0.0).astype(BF16),
                                       jnp.where(first_group, 0.0, vt).astype(BF16)], axis=0)
                tiles.append(_dot(wsp_ref[k], rhs))
            mixed.append(jnp.concatenate(tiles, axis=1) + bsb_ref[...])
        mixed = jnp.concatenate(mixed, axis=0)
        yb = _dot((uv[:, :SGU_WIDTH] * mixed).astype(BF16), wpb_ref[...])
        gb = jax.nn.sigmoid(_dot(hs, win_gb_ref[...]))
        ya = _dot(_lane_tiles(ya_scr, rows, SSM_TILES).astype(BF16), wpa_ref[...])
        ga = jax.nn.sigmoid(_dot(hs, win_ga_ref[...]))
        m = ga * ya + gb * yb
        x1 = x_slab(j) + _dot(m.astype(BF16), wout_ref[...])
        for e in range(SLAB_BATCHES):
            for k in range(D_TILES):
                o_ref[k, pl.ds(j * SLAB_BATCHES + e, CHUNK, stride=BATCH), :] = (
                    x1[e * CHUNK:(e + 1) * CHUNK, k * LANES:(k + 1) * LANES])
        return c

    lax.fori_loop(0, ROWS // SLAB, merge, 0)


def _const_spec(shape):
    nd = len(shape)
    return pl.BlockSpec(shape, lambda i, nd=nd: (0,) * nd, pipeline_mode=pl.Buffered(1))


def _mixer(x, consts):
    return pl.pallas_call(
        _mixer_kernel,
        grid=(N_STEPS,),
        in_specs=[pl.BlockSpec((BATCH, CHUNK, D_MODEL), lambda i: (0, i, 0))]
        + [_const_spec(c.shape) for c in consts],
        out_specs=pl.BlockSpec((D_TILES, ROWS, LANES), lambda i: (0, i, 0)),
        out_shape=jax.ShapeDtypeStruct((D_TILES, SEQ * BATCH, LANES), F32),
        scratch_shapes=[
            pltpu.VMEM((ROWS, D_MODEL), BF16),
            pltpu.VMEM((BATCH, STATE_COLS), F32),
            pltpu.VMEM((BATCH, STATE_COLS), F32),
            pltpu.VMEM((SCAN_ROWS, STATE_COLS), F32),
            pltpu.VMEM((SCAN_ROWS, STATE_COLS), F32),
            pltpu.VMEM((SSM_TILES, ROWS, LANES), F32),
            pltpu.VMEM((SSM_TILES, ROWS, LANES), F32),
        ],
        compiler_params=pltpu.CompilerParams(
            dimension_semantics=("arbitrary",), vmem_limit_bytes=VMEM_LIMIT),
        name="mixer",
    )(x, *consts)


def _ffn_kernel(x_ref, gffn_ref, wup_ref, cw_ref, cb_ref, wd_ref, gfin_ref, o_ref,
                h_bf, acc, carry, fin_scr):
    @pl.when(pl.program_id(0) == 0)
    def _():
        carry[...] = jnp.zeros_like(carry)

    def norm_slab(j, c):
        rows = pl.ds(pl.multiple_of(j * SLAB, SLAB), SLAB)
        h_bf[rows, :] = _rms(_lane_tiles(x_ref, rows, D_TILES), gffn_ref[...]).astype(BF16)
        return c

    lax.fori_loop(0, ROWS // SLAB, norm_slab, 0)

    def conv_tile(y, cols):
        prev = carry[:, cols]
        carry[:, cols] = y[ROWS - 2 * BATCH:, :]
        y1 = jnp.concatenate([prev[BATCH:, :], y[:ROWS - BATCH, :]], axis=0)
        y2 = jnp.concatenate([prev, y[:ROWS - 2 * BATCH, :]], axis=0)
        return (y2 * cw_ref[0:1, cols] + y1 * cw_ref[1:2, cols] + y * cw_ref[2:3, cols]
                + cb_ref[:, cols])

    hb = h_bf[...]

    def cols_a(c):
        return slice(c * FF_TILE, (c + 1) * FF_TILE)

    def cols_b(c):
        return slice(D_FF + c * FF_TILE, D_FF + (c + 1) * FF_TILE)

    def up(c):
        return _dot(hb, wup_ref[:, cols_a(c)]), _dot(hb, wup_ref[:, cols_b(c)])

    nxt = up(0)
    for c in range(N_FF_TILES):
        ya, yb = nxt
        if c + 1 < N_FF_TILES:
            nxt = up(c + 1)
        a = conv_tile(ya, cols_a(c))
        b = conv_tile(yb, cols_b(c))
        gated = (jax.nn.silu(a) * b).astype(BF16)
        down = _dot(gated, wd_ref[c * FF_TILE:(c + 1) * FF_TILE, :])
        if c == 0:
            acc[...] = down
        else:
            acc[...] += down

    def out_slab(j, c):
        rows = pl.ds(pl.multiple_of(j * SLAB, SLAB), SLAB)
        y = _rms(_lane_tiles(x_ref, rows, D_TILES) + acc[rows, :], gfin_ref[...])
        for k in range(D_TILES):
            fin_scr[k, rows, :] = y[:, k * LANES:(k + 1) * LANES]
        return c

    lax.fori_loop(0, ROWS // SLAB, out_slab, 0)

    for b in range(BATCH):
        for k in range(D_TILES):
            o_ref[b, :, k * LANES:(k + 1) * LANES] = fin_scr[k, pl.ds(b, CHUNK, stride=BATCH), :]


def _ffn(x1, consts):
    return pl.pallas_call(
        _ffn_kernel,
        grid=(N_STEPS,),
        in_specs=[pl.BlockSpec((D_TILES, ROWS, LANES), lambda i: (0, i, 0))]
        + [_const_spec(c.shape) for c in consts],
        out_specs=pl.BlockSpec((BATCH, CHUNK, D_MODEL), lambda i: (0, i, 0)),
        out_shape=jax.ShapeDtypeStruct((BATCH, SEQ, D_MODEL), F32),
        scratch_shapes=[
            pltpu.VMEM((ROWS, D_MODEL), BF16),
            pltpu.VMEM((ROWS, D_MODEL), F32),
            pltpu.VMEM((2 * BATCH, 2 * D_FF), F32),
            pltpu.VMEM((D_TILES, ROWS, LANES), F32),
        ],
        compiler_params=pltpu.CompilerParams(
            dimension_semantics=("arbitrary",), vmem_limit_bytes=VMEM_LIMIT),
        name="ffn",
    )(x1, *consts)


def _block_diag(blocks):
    g, r, c = blocks.shape
    eye = jnp.eye(g, dtype=blocks.dtype)
    return jnp.einsum("grc,gk->grkc", blocks, eye).reshape(g * r, g * c)


def kernel(x, g_mix, w_in, a_re, a_im, log_dt, b_re, b_im, c_re, c_im, d_skip, w_glu, b_glu, w_proj_a, g_sgu, w_s, b_s, w_proj_b, w_out, g_ffn, w_up, conv_w, conv_b, w_down, g_final):
    l = 0
    abar_re, abar_im, bbar_re, bbar_im, ws_m = _prep(
        a_re[l], a_im[l], log_dt[l][:, None],
        b_re[l].transpose(0, 2, 1), b_im[l].transpose(0, 2, 1), w_s[l])
    are = jnp.broadcast_to(abar_re.reshape(1, STATE_COLS), (BATCH, STATE_COLS))
    aim = jnp.broadcast_to(abar_im.reshape(1, STATE_COLS), (BATCH, STATE_COLS))
    bd_re = _block_diag(bbar_re)
    bd_im = _block_diag(bbar_im)
    bblk = jnp.stack([
        jnp.concatenate([bd_re[k * U_BLK:(k + 1) * U_BLK, k * ST_BLK:(k + 1) * ST_BLK],
                         bd_im[k * U_BLK:(k + 1) * U_BLK, k * ST_BLK:(k + 1) * ST_BLK]], axis=1)
        for k in range(N_BLK)]).astype(BF16)
    cd_re = _block_diag(c_re[l].transpose(0, 2, 1))
    cd_im = _block_diag(c_im[l].transpose(0, 2, 1))
    cblk = jnp.stack([
        jnp.concatenate([cd_re[k * ST_BLK:(k + 1) * ST_BLK, k * U_BLK:(k + 1) * U_BLK],
                         -cd_im[k * ST_BLK:(k + 1) * ST_BLK, k * U_BLK:(k + 1) * U_BLK]], axis=0)
        for k in range(N_BLK)]).astype(BF16)
    win = w_in[l].astype(BF16)
    o1 = SSM_WIDTH
    o2 = o1 + 2 * SGU_WIDTH
    o3 = o2 + D_MODEL
    wsp = ws_m.reshape(SGU_TILES, 2, CHUNK, CHUNK).transpose(0, 2, 1, 3).reshape(
        SGU_TILES, CHUNK, 2 * CHUNK).astype(BF16)
    bsb = jnp.repeat(b_s[l].T, SGU_GROUP_DIM, axis=1)
    mixer_consts = [
        g_mix[l][None, :], win[:, :o1], win[:, o1:o2], win[:, o2:o3], win[:, o3:],
        bblk, are, aim, cblk, d_skip[l][None, :], w_glu[l].astype(BF16), b_glu[l][None, :],
        w_proj_a[l].astype(BF16), g_sgu[l][None, :], wsp, bsb,
        w_proj_b[l].astype(BF16), w_out[l].astype(BF16),
    ]
    ffn_consts = [
        g_ffn[l][None, :], w_up[l].astype(BF16), conv_w[l], conv_b[l][None, :],
        w_down[l].astype(BF16), g_final[None, :],
    ]
    x1 = _mixer(x, mixer_consts)
    return _ffn(x1, ffn_consts)
```

```python
import jax
import jax.numpy as jnp
from jax import lax
from jax.experimental import pallas as pl
from jax.experimental.pallas import tpu as pltpu

D_MODEL = 1024
BATCH = 8
SEQ = 4096
SSM_WIDTH = 512
SSM_GROUP = 16
SSM_GROUPS = 32
SSM_STATE = 64
STATE_COLS = SSM_GROUPS * SSM_STATE
SGU_WIDTH = 512
SGU_GROUPS = 8
SGU_GROUP_DIM = 64
CHUNK = 128
LANES = 128
D_TILES = D_MODEL // LANES
SSM_TILES = SSM_WIDTH // LANES
SGU_TILES = SGU_WIDTH // LANES
D_FF = 2816
CONV_WIDTH = 3
EPS = 1e-6

ROWS = CHUNK * BATCH
N_STEPS = SEQ // CHUNK
SLAB = 256
SLAB_BATCHES = SLAB // CHUNK
SCAN_STEPS = 16
SCAN_ROWS = SCAN_STEPS * BATCH
SCAN_COLS = 1024
U_BLK = 128
ST_BLK = U_BLK // SSM_GROUP * SSM_STATE
N_BLK = SSM_WIDTH // U_BLK
FF_TILE = 256
N_FF_TILES = D_FF // FF_TILE
VMEM_LIMIT = 60 * 1024 * 1024

F32 = jnp.float32
BF16 = jnp.bfloat16


def _dot(a, b):
    return jnp.dot(a, b, preferred_element_type=F32)


def _rms(x, g):
    return x * lax.rsqrt(jnp.mean(x * x, axis=-1, keepdims=True) + EPS) * g


def _lane_tiles(ref, rows, n):
    return jnp.concatenate([ref[k, rows, :] for k in range(n)], axis=1)


def _prep_kernel(are_ref, aim_ref, ldt_ref, bre_ref, bim_ref, ws_ref,
                 abre_ref, abim_ref, bbre_ref, bbim_ref, wsm_ref):
    dt = jnp.exp(ldt_ref[...])
    ar = are_ref[...]
    ai = aim_ref[...]
    mag = jnp.exp(dt * ar)
    abar_re = mag * jnp.cos(dt * ai)
    abar_im = mag * jnp.sin(dt * ai)
    den = ar * ar + ai * ai
    nr = abar_re - 1.0
    ni = abar_im
    f_re = (nr * ar + ni * ai) / den
    f_im = (ni * ar - nr * ai) / den
    abre_ref[...] = abar_re
    abim_ref[...] = abar_im
    br = bre_ref[...]
    bi = bim_ref[...]
    fr = f_re[:, None, :]
    fi = f_im[:, None, :]
    bbre_ref[...] = fr * br - fi * bi
    bbim_ref[...] = fr * bi + fi * br
    row = lax.broadcasted_iota(jnp.int32, (SGU_GROUPS, CHUNK, CHUNK), 1)
    col = lax.broadcasted_iota(jnp.int32, (SGU_GROUPS, CHUNK, CHUNK), 2)
    wsm_ref[...] = jnp.where(row >= col, ws_ref[...], 0.0)


def _prep(a_re, a_im, log_dt, b_re_t, b_im_t, w_s):
    g, p, h = SSM_GROUPS, SSM_STATE, SSM_GROUP
    return pl.pallas_call(
        _prep_kernel,
        out_shape=(
            jax.ShapeDtypeStruct((g, p), F32),
            jax.ShapeDtypeStruct((g, p), F32),
            jax.ShapeDtypeStruct((g, h, p), F32),
            jax.ShapeDtypeStruct((g, h, p), F32),
            jax.ShapeDtypeStruct((SGU_GROUPS, CHUNK, CHUNK), F32),
        ),
        name="prep",
    )(a_re, a_im, log_dt, b_re_t, b_im_t, w_s)


def _mixer_kernel(x_ref, gmix_ref, win_u_ref, win_uv_ref, win_ga_ref, win_gb_ref,
                  bblk_ref, are_ref, aim_ref, cblk_ref, dskip_ref, wglu_ref, bglu_ref,
                  wpa_ref, gsgu_ref, wsp_ref, bsb_ref, wpb_ref, wout_ref,
                  o_ref,
                  h_bf, st_re, st_im, bu_re, bu_im, u_scr, ya_scr):
    @pl.when(pl.program_id(0) == 0)
    def _():
        st_re[...] = jnp.zeros_like(st_re)
        st_im[...] = jnp.zeros_like(st_im)

    def x_slab(j):
        return x_ref[pl.ds(j * SLAB_BATCHES, SLAB_BATCHES)].reshape(SLAB, D_MODEL)

    def norm_slab(j, c):
        rows = pl.ds(pl.multiple_of(j * SLAB, SLAB), SLAB)
        h = _rms(x_slab(j), gmix_ref[...]).astype(BF16)
        h_bf[rows, :] = h
        u = _dot(h, win_u_ref[...])
        for e in range(SLAB_BATCHES):
            for k in range(SSM_TILES):
                u_scr[k, pl.ds(j * SLAB_BATCHES + e, CHUNK, stride=BATCH), :] = (
                    u[e * CHUNK:(e + 1) * CHUNK, k * LANES:(k + 1) * LANES])
        return c

    lax.fori_loop(0, ROWS // SLAB, norm_slab, 0)

    def sub_rows(j):
        return slice(j * SCAN_ROWS, (j + 1) * SCAN_ROWS)

    def load_u(j):
        return _lane_tiles(u_scr, sub_rows(j), SSM_TILES)

    def project_in(j):
        slot = j % 2
        ub = load_u(j).astype(BF16)
        for k in range(N_BLK):
            bu = _dot(ub[:, k * U_BLK:(k + 1) * U_BLK], bblk_ref[k])
            bu_re[slot, :, k * ST_BLK:(k + 1) * ST_BLK] = bu[:, :ST_BLK]
            bu_im[slot, :, k * ST_BLK:(k + 1) * ST_BLK] = bu[:, ST_BLK:]

    def recur(j, state):
        slot = j % 2
        state = list(state)
        for t in range(SCAN_STEPS):
            rr = slice(t * BATCH, (t + 1) * BATCH)
            for cb in range(STATE_COLS // SCAN_COLS):
                cs = slice(cb * SCAN_COLS, (cb + 1) * SCAN_COLS)
                sre, sim = state[cb]
                ar = are_ref[:, cs]
                ai = aim_ref[:, cs]
                nre = ar * sre - ai * sim + bu_re[slot, rr, cs]
                nim = ar * sim + ai * sre + bu_im[slot, rr, cs]
                bu_re[slot, rr, cs] = nre
                bu_im[slot, rr, cs] = nim
                state[cb] = (nre, nim)
        return state

    def project_out(j):
        slot = j % 2
        ys = []
        for k in range(N_BLK):
            sr = bu_re[slot, :, k * ST_BLK:(k + 1) * ST_BLK].astype(BF16)
            si = bu_im[slot, :, k * ST_BLK:(k + 1) * ST_BLK].astype(BF16)
            ys.append(_dot(sr, cblk_ref[k, :ST_BLK, :]) + _dot(si, cblk_ref[k, ST_BLK:, :]))
        y = jnp.concatenate(ys, axis=1) + dskip_ref[...] * load_u(j)
        y = jax.nn.gelu(y)
        y = y * jax.nn.sigmoid(_dot(y.astype(BF16), wglu_ref[...]) + bglu_ref[...])
        for k in range(SSM_TILES):
            ya_scr[k, sub_rows(j), :] = y[:, k * LANES:(k + 1) * LANES]

    n_sub = CHUNK // SCAN_STEPS
    state = [(st_re[:, cb * SCAN_COLS:(cb + 1) * SCAN_COLS],
              st_im[:, cb * SCAN_COLS:(cb + 1) * SCAN_COLS])
             for cb in range(STATE_COLS // SCAN_COLS)]
    project_in(0)
    for j in range(n_sub):
        if j > 0:
            project_out(j - 1)
        if j + 1 < n_sub:
            project_in(j + 1)
        state = recur(j, state)
    project_out(n_sub - 1)
    for cb, (sre, sim) in enumerate(state):
        st_re[:, cb * SCAN_COLS:(cb + 1) * SCAN_COLS] = sre
        st_im[:, cb * SCAN_COLS:(cb + 1) * SCAN_COLS] = sim

    lane = lax.broadcasted_iota(jnp.int32, (CHUNK, LANES), 1)
    first_group = lane < SGU_GROUP_DIM

    def merge(j, c):
        rows = pl.ds(pl.multiple_of(j * SLAB, SLAB), SLAB)
        hs = h_bf[rows, :]
        uv = jax.nn.gelu(_dot(hs, win_uv_ref[...]))
        v = _rms(uv[:, SGU_WIDTH:], gsgu_ref[...])
        mixed = []
        for e in range(SLAB_BATCHES):
            tiles = []
            for k in range(SGU_TILES):
                vt = v[e * CHUNK:(e + 1) * CHUNK, k * LANES:(k + 1) * LANES]
                rhs = jnp.concatenate([jnp.where(first_group, vt, 0.0).astype(BF16),
                                       jnp.where(first_group, 0.0, vt).astype(BF16)], axis=0)
                tiles.append(_dot(wsp_ref[k], rhs))
            mixed.append(jnp.concatenate(tiles, axis=1) + bsb_ref[...])
        mixed = jnp.concatenate(mixed, axis=0)
        yb = _dot((uv[:, :SGU_WIDTH] * mixed).astype(BF16), wpb_ref[...])
        gb = jax.nn.sigmoid(_dot(hs, win_gb_ref[...]))
        ya_in = jnp.concatenate([
            _lane_tiles(ya_scr, pl.ds(j * SLAB_BATCHES + e, CHUNK, stride=BATCH), SSM_TILES)
            for e in range(SLAB_BATCHES)], axis=0)
        ya = _dot(ya_in.astype(BF16), wpa_ref[...])
        ga = jax.nn.sigmoid(_dot(hs, win_ga_ref[...]))
        m = ga * ya + gb * yb
        x1 = x_slab(j) + _dot(m.astype(BF16), wout_ref[...])
        for e in range(SLAB_BATCHES):
            for k in range(D_TILES):
                o_ref[k, pl.ds(j * SLAB_BATCHES + e, CHUNK, stride=BATCH), :] = (
                    x1[e * CHUNK:(e + 1) * CHUNK, k * LANES:(k + 1) * LANES])
        return c

    lax.fori_loop(0, ROWS // SLAB, merge, 0)


def _const_spec(shape):
    nd = len(shape)
    return pl.BlockSpec(shape, lambda i, nd=nd: (0,) * nd, pipeline_mode=pl.Buffered(1))


def _mixer(x, consts):
    return pl.pallas_call(
        _mixer_kernel,
        grid=(N_STEPS,),
        in_specs=[pl.BlockSpec((BATCH, CHUNK, D_MODEL), lambda i: (0, i, 0))]
        + [_const_spec(c.shape) for c in consts],
        out_specs=pl.BlockSpec((D_TILES, ROWS, LANES), lambda i: (0, i, 0)),
        out_shape=jax.ShapeDtypeStruct((D_TILES, SEQ * BATCH, LANES), F32),
        scratch_shapes=[
            pltpu.VMEM((ROWS, D_MODEL), BF16),
            pltpu.VMEM((BATCH, STATE_COLS), F32),
            pltpu.VMEM((BATCH, STATE_COLS), F32),
            pltpu.VMEM((2, SCAN_ROWS, STATE_COLS), F32),
            pltpu.VMEM((2, SCAN_ROWS, STATE_COLS), F32),
            pltpu.VMEM((SSM_TILES, ROWS, LANES), F32),
            pltpu.VMEM((SSM_TILES, ROWS, LANES), F32),
        ],
        compiler_params=pltpu.CompilerParams(
            dimension_semantics=("arbitrary",), vmem_limit_bytes=VMEM_LIMIT),
        name="mixer",
    )(x, *consts)


def _ffn_kernel(x_ref, gffn_ref, wup_ref, cw_ref, cb_ref, wd_ref, gfin_ref, o_ref,
                h_bf, acc, carry, fin_scr):
    @pl.when(pl.program_id(0) == 0)
    def _():
        carry[...] = jnp.zeros_like(carry)

    def norm_slab(j, c):
        rows = pl.ds(pl.multiple_of(j * SLAB, SLAB), SLAB)
        h_bf[rows, :] = _rms(_lane_tiles(x_ref, rows, D_TILES), gffn_ref[...]).astype(BF16)
        return c

    lax.fori_loop(0, ROWS // SLAB, norm_slab, 0)

    def conv_tile(y, cols):
        prev = carry[:, cols]
        carry[:, cols] = y[ROWS - 2 * BATCH:, :]
        y1 = jnp.concatenate([prev[BATCH:, :], y[:ROWS - BATCH, :]], axis=0)
        y2 = jnp.concatenate([prev, y[:ROWS - 2 * BATCH, :]], axis=0)
        return (y2 * cw_ref[0:1, cols] + y1 * cw_ref[1:2, cols] + y * cw_ref[2:3, cols]
                + cb_ref[:, cols])

    hb = h_bf[...]

    def cols_a(c):
        return slice(c * FF_TILE, (c + 1) * FF_TILE)

    def cols_b(c):
        return slice(D_FF + c * FF_TILE, D_FF + (c + 1) * FF_TILE)

    def up(c):
        return _dot(hb, wup_ref[:, cols_a(c)]), _dot(hb, wup_ref[:, cols_b(c)])

    nxt = up(0)
    for c in range(N_FF_TILES):
        ya, yb = nxt
        if c + 1 < N_FF_TILES:
            nxt = up(c + 1)
        a = conv_tile(ya, cols_a(c))
        b = conv_tile(yb, cols_b(c))
        gated = (jax.nn.silu(a) * b).astype(BF16)
        down = _dot(gated, wd_ref[c * FF_TILE:(c + 1) * FF_TILE, :])
        if c == 0:
            acc[...] = down
        else:
            acc[...] += down

    def out_slab(j, c):
        rows = pl.ds(pl.multiple_of(j * SLAB, SLAB), SLAB)
        y = _rms(_lane_tiles(x_ref, rows, D_TILES) + acc[rows, :], gfin_ref[...])
        for k in range(D_TILES):
            fin_scr[k, rows, :] = y[:, k * LANES:(k + 1) * LANES]
        return c

    lax.fori_loop(0, ROWS // SLAB, out_slab, 0)

    for b in range(BATCH):
        for k in range(D_TILES):
            o_ref[b, :, k * LANES:(k + 1) * LANES] = fin_scr[k, pl.ds(b, CHUNK, stride=BATCH), :]


def _ffn(x1, consts):
    return pl.pallas_call(
        _ffn_kernel,
        grid=(N_STEPS,),
        in_specs=[pl.BlockSpec((D_TILES, ROWS, LANES), lambda i: (0, i, 0))]
        + [_const_spec(c.shape) for c in consts],
        out_specs=pl.BlockSpec((BATCH, CHUNK, D_MODEL), lambda i: (0, i, 0)),
        out_shape=jax.ShapeDtypeStruct((BATCH, SEQ, D_MODEL), F32),
        scratch_shapes=[
            pltpu.VMEM((ROWS, D_MODEL), BF16),
            pltpu.VMEM((ROWS, D_MODEL), F32),
            pltpu.VMEM((2 * BATCH, 2 * D_FF), F32),
            pltpu.VMEM((D_TILES, ROWS, LANES), F32),
        ],
        compiler_params=pltpu.CompilerParams(
            dimension_semantics=("arbitrary",), vmem_limit_bytes=VMEM_LIMIT),
        name="ffn",
    )(x1, *consts)


def _block_diag(blocks):
    g, r, c = blocks.shape
    eye = jnp.eye(g, dtype=blocks.dtype)
    return jnp.einsum("grc,gk->grkc", blocks, eye).reshape(g * r, g * c)


def kernel(x, g_mix, w_in, a_re, a_im, log_dt, b_re, b_im, c_re, c_im, d_skip, w_glu, b_glu, w_proj_a, g_sgu, w_s, b_s, w_proj_b, w_out, g_ffn, w_up, conv_w, conv_b, w_down, g_final):
    l = 0
    abar_re, abar_im, bbar_re, bbar_im, ws_m = _prep(
        a_re[l], a_im[l], log_dt[l][:, None],
        b_re[l].transpose(0, 2, 1), b_im[l].transpose(0, 2, 1), w_s[l])
    are = jnp.broadcast_to(abar_re.reshape(1, STATE_COLS), (BATCH, STATE_COLS))
    aim = jnp.broadcast_to(abar_im.reshape(1, STATE_COLS), (BATCH, STATE_COLS))
    bd_re = _block_diag(bbar_re)
    bd_im = _block_diag(bbar_im)
    bblk = jnp.stack([
        jnp.concatenate([bd_re[k * U_BLK:(k + 1) * U_BLK, k * ST_BLK:(k + 1) * ST_BLK],
                         bd_im[k * U_BLK:(k + 1) * U_BLK, k * ST_BLK:(k + 1) * ST_BLK]], axis=1)
        for k in range(N_BLK)]).astype(BF16)
    cd_re = _block_diag(c_re[l].transpose(0, 2, 1))
    cd_im = _block_diag(c_im[l].transpose(0, 2, 1))
    cblk = jnp.stack([
        jnp.concatenate([cd_re[k * ST_BLK:(k + 1) * ST_BLK, k * U_BLK:(k + 1) * U_BLK],
                         -cd_im[k * ST_BLK:(k + 1) * ST_BLK, k * U_BLK:(k + 1) * U_BLK]], axis=0)
        for k in range(N_BLK)]).astype(BF16)
    win = w_in[l].astype(BF16)
    o1 = SSM_WIDTH
    o2 = o1 + 2 * SGU_WIDTH
    o3 = o2 + D_MODEL
    wsp = ws_m.reshape(SGU_TILES, 2, CHUNK, CHUNK).transpose(0, 2, 1, 3).reshape(
        SGU_TILES, CHUNK, 2 * CHUNK).astype(BF16)
    bsb = jnp.repeat(b_s[l].T, SGU_GROUP_DIM, axis=1)
    mixer_consts = [
        g_mix[l][None, :], win[:, :o1], win[:, o1:o2], win[:, o2:o3], win[:, o3:],
        bblk, are, aim, cblk, d_skip[l][None, :], w_glu[l].astype(BF16), b_glu[l][None, :],
        w_proj_a[l].astype(BF16), g_sgu[l][None, :], wsp, bsb,
        w_proj_b[l].astype(BF16), w_out[l].astype(BF16),
    ]
    ffn_consts = [
        g_ffn[l][None, :], w_up[l].astype(BF16), conv_w[l], conv_b[l][None, :],
        w_down[l].astype(BF16), g_final[None, :],
    ]
    x1 = _mixer(x, mixer_consts)
    return _ffn(x1, ffn_consts)
```

```python
import jax
import jax.numpy as jnp
from jax import lax
from jax.experimental import pallas as pl
from jax.experimental.pallas import tpu as pltpu

D_MODEL = 1024
BATCH = 8
SEQ = 4096
SSM_WIDTH = 512
SSM_GROUP = 16
SSM_GROUPS = 32
SSM_STATE = 64
STATE_COLS = SSM_GROUPS * SSM_STATE
SGU_WIDTH = 512
SGU_GROUPS = 8
SGU_GROUP_DIM = 64
CHUNK = 128
LANES = 128
D_TILES = D_MODEL // LANES
SSM_TILES = SSM_WIDTH // LANES
SGU_TILES = SGU_WIDTH // LANES
D_FF = 2816
CONV_WIDTH = 3
EPS = 1e-6

ROWS = CHUNK * BATCH
N_STEPS = SEQ // CHUNK
SLAB = 256
SLAB_BATCHES = SLAB // CHUNK
SCAN_STEPS = 16
SCAN_ROWS = SCAN_STEPS * BATCH
SCAN_COLS = 1024
U_BLK = 128
ST_BLK = U_BLK // SSM_GROUP * SSM_STATE
N_BLK = SSM_WIDTH // U_BLK
FF_TILE = 512
FF_BOUNDS = list(range(0, D_FF, FF_TILE)) + [D_FF]
N_FF_TILES = len(FF_BOUNDS) - 1
VMEM_LIMIT = 60 * 1024 * 1024

F32 = jnp.float32
BF16 = jnp.bfloat16


def _dot(a, b):
    return jnp.dot(a, b, preferred_element_type=F32)


def _rms(x, g):
    return x * lax.rsqrt(jnp.mean(x * x, axis=-1, keepdims=True) + EPS) * g


def _lane_tiles(ref, rows, n):
    return jnp.concatenate([ref[k, rows, :] for k in range(n)], axis=1)


def _prep_kernel(are_ref, aim_ref, ldt_ref, bre_ref, bim_ref, ws_ref,
                 abre_ref, abim_ref, bbre_ref, bbim_ref, wsm_ref):
    dt = jnp.exp(ldt_ref[...])
    ar = are_ref[...]
    ai = aim_ref[...]
    mag = jnp.exp(dt * ar)
    abar_re = mag * jnp.cos(dt * ai)
    abar_im = mag * jnp.sin(dt * ai)
    den = ar * ar + ai * ai
    nr = abar_re - 1.0
    ni = abar_im
    f_re = (nr * ar + ni * ai) / den
    f_im = (ni * ar - nr * ai) / den
    abre_ref[...] = abar_re
    abim_ref[...] = abar_im
    br = bre_ref[...]
    bi = bim_ref[...]
    fr = f_re[:, None, :]
    fi = f_im[:, None, :]
    bbre_ref[...] = fr * br - fi * bi
    bbim_ref[...] = fr * bi + fi * br
    row = lax.broadcasted_iota(jnp.int32, (SGU_GROUPS, CHUNK, CHUNK), 1)
    col = lax.broadcasted_iota(jnp.int32, (SGU_GROUPS, CHUNK, CHUNK), 2)
    wsm_ref[...] = jnp.where(row >= col, ws_ref[...], 0.0)


def _prep(a_re, a_im, log_dt, b_re_t, b_im_t, w_s):
    g, p, h = SSM_GROUPS, SSM_STATE, SSM_GROUP
    return pl.pallas_call(
        _prep_kernel,
        out_shape=(
            jax.ShapeDtypeStruct((g, p), F32),
            jax.ShapeDtypeStruct((g, p), F32),
            jax.ShapeDtypeStruct((g, h, p), F32),
            jax.ShapeDtypeStruct((g, h, p), F32),
            jax.ShapeDtypeStruct((SGU_GROUPS, CHUNK, CHUNK), F32),
        ),
        name="prep",
    )(a_re, a_im, log_dt, b_re_t, b_im_t, w_s)


def _mixer_kernel(x_ref, gmix_ref, win_u_ref, win_uv_ref, win_ga_ref, win_gb_ref,
                  bblk_ref, are_ref, aim_ref, cblk_ref, dskip_ref, wglu_ref, bglu_ref,
                  wpa_ref, gsgu_ref, wsp_ref, bsb_ref, wpb_ref, wout_ref,
                  o_ref,
                  h_bf, st_re, st_im, bu_re, bu_im, u_scr, ya_scr):
    @pl.when(pl.program_id(0) == 0)
    def _():
        st_re[...] = jnp.zeros_like(st_re)
        st_im[...] = jnp.zeros_like(st_im)

    def x_slab(j):
        return x_ref[pl.ds(j * SLAB_BATCHES, SLAB_BATCHES)].reshape(SLAB, D_MODEL)

    def norm_slab(j, c):
        rows = pl.ds(pl.multiple_of(j * SLAB, SLAB), SLAB)
        h = _rms(x_slab(j), gmix_ref[...]).astype(BF16)
        h_bf[rows, :] = h
        u = _dot(h, win_u_ref[...])
        for e in range(SLAB_BATCHES):
            for k in range(SSM_TILES):
                u_scr[k, pl.ds(j * SLAB_BATCHES + e, CHUNK, stride=BATCH), :] = (
                    u[e * CHUNK:(e + 1) * CHUNK, k * LANES:(k + 1) * LANES])
        return c

    lax.fori_loop(0, ROWS // SLAB, norm_slab, 0)

    def sub_rows(j):
        return slice(j * SCAN_ROWS, (j + 1) * SCAN_ROWS)

    def load_u(j):
        return _lane_tiles(u_scr, sub_rows(j), SSM_TILES)

    def project_in(j):
        slot = j % 2
        ub = load_u(j).astype(BF16)
        for k in range(N_BLK):
            bu = _dot(ub[:, k * U_BLK:(k + 1) * U_BLK], bblk_ref[k])
            bu_re[slot, :, k * ST_BLK:(k + 1) * ST_BLK] = bu[:, :ST_BLK]
            bu_im[slot, :, k * ST_BLK:(k + 1) * ST_BLK] = bu[:, ST_BLK:]

    def recur(j, state):
        slot = j % 2
        state = list(state)
        for t in range(SCAN_STEPS):
            rr = slice(t * BATCH, (t + 1) * BATCH)
            for cb in range(STATE_COLS // SCAN_COLS):
                cs = slice(cb * SCAN_COLS, (cb + 1) * SCAN_COLS)
                sre, sim = state[cb]
                ar = are_ref[:, cs]
                ai = aim_ref[:, cs]
                nre = ar * sre - ai * sim + bu_re[slot, rr, cs]
                nim = ar * sim + ai * sre + bu_im[slot, rr, cs]
                bu_re[slot, rr, cs] = nre
                bu_im[slot, rr, cs] = nim
                state[cb] = (nre, nim)
        return state

    def project_out(j):
        slot = j % 2
        ys = []
        for k in range(N_BLK):
            sr = bu_re[slot, :, k * ST_BLK:(k + 1) * ST_BLK].astype(BF16)
            si = bu_im[slot, :, k * ST_BLK:(k + 1) * ST_BLK].astype(BF16)
            ys.append(_dot(sr, cblk_ref[k, :ST_BLK, :]) + _dot(si, cblk_ref[k, ST_BLK:, :]))
        y = jnp.concatenate(ys, axis=1) + dskip_ref[...] * load_u(j)
        y = jax.nn.gelu(y)
        y = y * jax.nn.sigmoid(_dot(y.astype(BF16), wglu_ref[...]) + bglu_ref[...])
        for k in range(SSM_TILES):
            ya_scr[k, sub_rows(j), :] = y[:, k * LANES:(k + 1) * LANES]

    n_sub = CHUNK // SCAN_STEPS
    state = [(st_re[:, cb * SCAN_COLS:(cb + 1) * SCAN_COLS],
              st_im[:, cb * SCAN_COLS:(cb + 1) * SCAN_COLS])
             for cb in range(STATE_COLS // SCAN_COLS)]
    project_in(0)
    for j in range(n_sub):
        if j > 0:
            project_out(j - 1)
        if j + 1 < n_sub:
            project_in(j + 1)
        state = recur(j, state)
    project_out(n_sub - 1)
    for cb, (sre, sim) in enumerate(state):
        st_re[:, cb * SCAN_COLS:(cb + 1) * SCAN_COLS] = sre
        st_im[:, cb * SCAN_COLS:(cb + 1) * SCAN_COLS] = sim

    lane = lax.broadcasted_iota(jnp.int32, (CHUNK, LANES), 1)
    first_group = lane < SGU_GROUP_DIM

    def slab_rows(j):
        return slice(j * SLAB, (j + 1) * SLAB)

    def gate_dots(j):
        hs = h_bf[slab_rows(j), :]
        return (_dot(hs, win_uv_ref[...]), _dot(hs, win_gb_ref[...]), _dot(hs, win_ga_ref[...]))

    def merge(j, uv, gb, ga):
        uv = jax.nn.gelu(uv)
        v = _rms(uv[:, SGU_WIDTH:], gsgu_ref[...])
        mixed = []
        for e in range(SLAB_BATCHES):
            tiles = []
            for k in range(SGU_TILES):
                vt = v[e * CHUNK:(e + 1) * CHUNK, k * LANES:(k + 1) * LANES]
                rhs = jnp.concatenate([jnp.where(first_group, vt, 0.0).astype(BF16),
                                       jnp.where(first_group, 0.0, vt).astype(BF16)], axis=0)
                tiles.append(_dot(wsp_ref[k], rhs))
            mixed.append(jnp.concatenate(tiles, axis=1) + bsb_ref[...])
        mixed = jnp.concatenate(mixed, axis=0)
        yb = _dot((uv[:, :SGU_WIDTH] * mixed).astype(BF16), wpb_ref[...])
        ya_in = jnp.concatenate([
            _lane_tiles(ya_scr, pl.ds(j * SLAB_BATCHES + e, CHUNK, stride=BATCH), SSM_TILES)
            for e in range(SLAB_BATCHES)], axis=0)
        ya = _dot(ya_in.astype(BF16), wpa_ref[...])
        m = jax.nn.sigmoid(ga) * ya + jax.nn.sigmoid(gb) * yb
        x1 = x_slab(j) + _dot(m.astype(BF16), wout_ref[...])
        for e in range(SLAB_BATCHES):
            for k in range(D_TILES):
                o_ref[k, pl.ds(j * SLAB_BATCHES + e, CHUNK, stride=BATCH), :] = (
                    x1[e * CHUNK:(e + 1) * CHUNK, k * LANES:(k + 1) * LANES])

    n_slabs = ROWS // SLAB
    nxt = gate_dots(0)
    for j in range(n_slabs):
        cur = nxt
        if j + 1 < n_slabs:
            nxt = gate_dots(j + 1)
        merge(j, *cur)


def _const_spec(shape):
    nd = len(shape)
    return pl.BlockSpec(shape, lambda i, nd=nd: (0,) * nd, pipeline_mode=pl.Buffered(1))


def _mixer(x, consts):
    return pl.pallas_call(
        _mixer_kernel,
        grid=(N_STEPS,),
        in_specs=[pl.BlockSpec((BATCH, CHUNK, D_MODEL), lambda i: (0, i, 0))]
        + [_const_spec(c.shape) for c in consts],
        out_specs=pl.BlockSpec((D_TILES, ROWS, LANES), lambda i: (0, i, 0)),
        out_shape=jax.ShapeDtypeStruct((D_TILES, SEQ * BATCH, LANES), F32),
        scratch_shapes=[
            pltpu.VMEM((ROWS, D_MODEL), BF16),
            pltpu.VMEM((BATCH, STATE_COLS), F32),
            pltpu.VMEM((BATCH, STATE_COLS), F32),
            pltpu.VMEM((2, SCAN_ROWS, STATE_COLS), F32),
            pltpu.VMEM((2, SCAN_ROWS, STATE_COLS), F32),
            pltpu.VMEM((SSM_TILES, ROWS, LANES), F32),
            pltpu.VMEM((SSM_TILES, ROWS, LANES), F32),
        ],
        compiler_params=pltpu.CompilerParams(
            dimension_semantics=("arbitrary",), vmem_limit_bytes=VMEM_LIMIT),
        name="mixer",
    )(x, *consts)


def _ffn_kernel(x_ref, gffn_ref, wup_ref, cw_ref, cb_ref, wd_ref, gfin_ref, o_ref,
                h_bf, acc, carry, fin_scr):
    @pl.when(pl.program_id(0) == 0)
    def _():
        carry[...] = jnp.zeros_like(carry)

    def norm_slab(j, c):
        rows = pl.ds(pl.multiple_of(j * SLAB, SLAB), SLAB)
        h_bf[rows, :] = _rms(_lane_tiles(x_ref, rows, D_TILES), gffn_ref[...]).astype(BF16)
        return c

    lax.fori_loop(0, ROWS // SLAB, norm_slab, 0)

    def conv_tile(y, cols):
        prev = carry[:, cols]
        carry[:, cols] = y[ROWS - 2 * BATCH:, :]
        y1 = jnp.concatenate([prev[BATCH:, :], y[:ROWS - BATCH, :]], axis=0)
        y2 = jnp.concatenate([prev, y[:ROWS - 2 * BATCH, :]], axis=0)
        return (y2 * cw_ref[0:1, cols] + y1 * cw_ref[1:2, cols] + y * cw_ref[2:3, cols]
                + cb_ref[:, cols])

    hb = h_bf[...]

    def cols_a(c):
        return slice(FF_BOUNDS[c], FF_BOUNDS[c + 1])

    def cols_b(c):
        return slice(D_FF + FF_BOUNDS[c], D_FF + FF_BOUNDS[c + 1])

    def up(c):
        return _dot(hb, wup_ref[:, cols_a(c)]), _dot(hb, wup_ref[:, cols_b(c)])

    nxt = up(0)
    for c in range(N_FF_TILES):
        ya, yb = nxt
        if c + 1 < N_FF_TILES:
            nxt = up(c + 1)
        a = conv_tile(ya, cols_a(c))
        b = conv_tile(yb, cols_b(c))
        gated = (jax.nn.silu(a) * b).astype(BF16)
        down = _dot(gated, wd_ref[cols_a(c), :])
        if c == 0:
            acc[...] = down
        else:
            acc[...] += down

    def out_slab(j, c):
        rows = pl.ds(pl.multiple_of(j * SLAB, SLAB), SLAB)
        y = _rms(_lane_tiles(x_ref, rows, D_TILES) + acc[rows, :], gfin_ref[...])
        for k in range(D_TILES):
            fin_scr[k, rows, :] = y[:, k * LANES:(k + 1) * LANES]
        return c

    lax.fori_loop(0, ROWS // SLAB, out_slab, 0)

    for b in range(BATCH):
        for k in range(D_TILES):
            o_ref[b, :, k * LANES:(k + 1) * LANES] = fin_scr[k, pl.ds(b, CHUNK, stride=BATCH), :]


def _ffn(x1, consts):
    return pl.pallas_call(
        _ffn_kernel,
        grid=(N_STEPS,),
        in_specs=[pl.BlockSpec((D_TILES, ROWS, LANES), lambda i: (0, i, 0))]
        + [_const_spec(c.shape) for c in consts],
        out_specs=pl.BlockSpec((BATCH, CHUNK, D_MODEL), lambda i: (0, i, 0)),
        out_shape=jax.ShapeDtypeStruct((BATCH, SEQ, D_MODEL), F32),
        scratch_shapes=[
            pltpu.VMEM((ROWS, D_MODEL), BF16),
            pltpu.VMEM((ROWS, D_MODEL), F32),
            pltpu.VMEM((2 * BATCH, 2 * D_FF), F32),
            pltpu.VMEM((D_TILES, ROWS, LANES), F32),
        ],
        compiler_params=pltpu.CompilerParams(
            dimension_semantics=("arbitrary",), vmem_limit_bytes=VMEM_LIMIT),
        name="ffn",
    )(x1, *consts)


def _block_diag(blocks):
    g, r, c = blocks.shape
    eye = jnp.eye(g, dtype=blocks.dtype)
    return jnp.einsum("grc,gk->grkc", blocks, eye).reshape(g * r, g * c)


def kernel(x, g_mix, w_in, a_re, a_im, log_dt, b_re, b_im, c_re, c_im, d_skip, w_glu, b_glu, w_proj_a, g_sgu, w_s, b_s, w_proj_b, w_out, g_ffn, w_up, conv_w, conv_b, w_down, g_final):
    l = 0
    abar_re, abar_im, bbar_re, bbar_im, ws_m = _prep(
        a_re[l], a_im[l], log_dt[l][:, None],
        b_re[l].transpose(0, 2, 1), b_im[l].transpose(0, 2, 1), w_s[l])
    are = jnp.broadcast_to(abar_re.reshape(1, STATE_COLS), (BATCH, STATE_COLS))
    aim = jnp.broadcast_to(abar_im.reshape(1, STATE_COLS), (BATCH, STATE_COLS))
    bd_re = _block_diag(bbar_re)
    bd_im = _block_diag(bbar_im)
    bblk = jnp.stack([
        jnp.concatenate([bd_re[k * U_BLK:(k + 1) * U_BLK, k * ST_BLK:(k + 1) * ST_BLK],
                         bd_im[k * U_BLK:(k + 1) * U_BLK, k * ST_BLK:(k + 1) * ST_BLK]], axis=1)
        for k in range(N_BLK)]).astype(BF16)
    cd_re = _block_diag(c_re[l].transpose(0, 2, 1))
    cd_im = _block_diag(c_im[l].transpose(0, 2, 1))
    cblk = jnp.stack([
        jnp.concatenate([cd_re[k * ST_BLK:(k + 1) * ST_BLK, k * U_BLK:(k + 1) * U_BLK],
                         -cd_im[k * ST_BLK:(k + 1) * ST_BLK, k * U_BLK:(k + 1) * U_BLK]], axis=0)
        for k in range(N_BLK)]).astype(BF16)
    win = w_in[l].astype(BF16)
    o1 = SSM_WIDTH
    o2 = o1 + 2 * SGU_WIDTH
    o3 = o2 + D_MODEL
    wsp = ws_m.reshape(SGU_TILES, 2, CHUNK, CHUNK).transpose(0, 2, 1, 3).reshape(
        SGU_TILES, CHUNK, 2 * CHUNK).astype(BF16)
    bsb = jnp.repeat(b_s[l].T, SGU_GROUP_DIM, axis=1)
    mixer_consts = [
        g_mix[l][None, :], win[:, :o1], win[:, o1:o2], win[:, o2:o3], win[:, o3:],
        bblk, are, aim, cblk, d_skip[l][None, :], w_glu[l].astype(BF16), b_glu[l][None, :],
        w_proj_a[l].astype(BF16), g_sgu[l][None, :], wsp, bsb,
        w_proj_b[l].astype(BF16), w_out[l].astype(BF16),
    ]
    ffn_consts = [
        g_ffn[l][None, :], w_up[l].astype(BF16), conv_w[l], conv_b[l][None, :],
        w_down[l].astype(BF16), g_final[None, :],
    ]
    x1 = _mixer(x, mixer_consts)
    return _ffn(x1, ffn_consts)
```

```python
import jax
import jax.numpy as jnp
from jax import lax
from jax.experimental import pallas as pl
from jax.experimental.pallas import tpu as pltpu

D_MODEL = 1024
BATCH = 8
SEQ = 4096
SSM_WIDTH = 512
SSM_GROUP = 16
SSM_GROUPS = 32
SSM_STATE = 64
STATE_COLS = SSM_GROUPS * SSM_STATE
SGU_WIDTH = 512
SGU_GROUPS = 8
SGU_GROUP_DIM = 64
CHUNK = 128
LANES = 128
D_TILES = D_MODEL // LANES
SSM_TILES = SSM_WIDTH // LANES
SGU_TILES = SGU_WIDTH // LANES
D_FF = 2816
CONV_WIDTH = 3
EPS = 1e-6

ROWS = CHUNK * BATCH
N_STEPS = SEQ // CHUNK
SLAB = 256
SLAB_BATCHES = SLAB // CHUNK
SCAN_STEPS = 16
SCAN_ROWS = SCAN_STEPS * BATCH
SCAN_COLS = 1024
U_BLK = 128
ST_BLK = U_BLK // SSM_GROUP * SSM_STATE
N_BLK = SSM_WIDTH // U_BLK
WIN_U = slice(0, SSM_WIDTH)
WIN_UV = slice(SSM_WIDTH, SSM_WIDTH + 2 * SGU_WIDTH)
WIN_GA = slice(SSM_WIDTH + 2 * SGU_WIDTH, SSM_WIDTH + 2 * SGU_WIDTH + D_MODEL)
WIN_GB = slice(SSM_WIDTH + 2 * SGU_WIDTH + D_MODEL, SSM_WIDTH + 2 * SGU_WIDTH + 2 * D_MODEL)
FF_TILE = 512
FF_BOUNDS = list(range(0, D_FF, FF_TILE)) + [D_FF]
N_FF_TILES = len(FF_BOUNDS) - 1
VMEM_LIMIT = 60 * 1024 * 1024

F32 = jnp.float32
BF16 = jnp.bfloat16


def _dot(a, b):
    return jnp.dot(a, b, preferred_element_type=F32)


def _rms(x, g):
    return x * lax.rsqrt(jnp.mean(x * x, axis=-1, keepdims=True) + EPS) * g


def _lane_tiles(ref, rows, n):
    return jnp.concatenate([ref[k, rows, :] for k in range(n)], axis=1)


def _prep_kernel(are_ref, aim_ref, ldt_ref, bre_ref, bim_ref, ws_ref,
                 abre_ref, abim_ref, bbre_ref, bbim_ref, wsm_ref):
    dt = jnp.exp(ldt_ref[...])
    ar = are_ref[...]
    ai = aim_ref[...]
    mag = jnp.exp(dt * ar)
    abar_re = mag * jnp.cos(dt * ai)
    abar_im = mag * jnp.sin(dt * ai)
    den = ar * ar + ai * ai
    nr = abar_re - 1.0
    ni = abar_im
    f_re = (nr * ar + ni * ai) / den
    f_im = (ni * ar - nr * ai) / den
    abre_ref[...] = abar_re
    abim_ref[...] = abar_im
    br = bre_ref[...]
    bi = bim_ref[...]
    fr = f_re[:, None, :]
    fi = f_im[:, None, :]
    bbre_ref[...] = fr * br - fi * bi
    bbim_ref[...] = fr * bi + fi * br
    row = lax.broadcasted_iota(jnp.int32, (SGU_GROUPS, CHUNK, CHUNK), 1)
    col = lax.broadcasted_iota(jnp.int32, (SGU_GROUPS, CHUNK, CHUNK), 2)
    wsm_ref[...] = jnp.where(row >= col, ws_ref[...], 0.0)


def _prep(a_re, a_im, log_dt, b_re_t, b_im_t, w_s):
    g, p, h = SSM_GROUPS, SSM_STATE, SSM_GROUP
    return pl.pallas_call(
        _prep_kernel,
        out_shape=(
            jax.ShapeDtypeStruct((g, p), F32),
            jax.ShapeDtypeStruct((g, p), F32),
            jax.ShapeDtypeStruct((g, h, p), F32),
            jax.ShapeDtypeStruct((g, h, p), F32),
            jax.ShapeDtypeStruct((SGU_GROUPS, CHUNK, CHUNK), F32),
        ),
        name="prep",
    )(a_re, a_im, log_dt, b_re_t, b_im_t, w_s)


def _mixer_kernel(x_ref, gmix_ref, win_ref,
                  bblk_ref, are_ref, aim_ref, cblk_ref, dskip_ref, wglu_ref, bglu_ref,
                  wpa_ref, gsgu_ref, wsp_ref, bsb_ref, wpb_ref, wout_ref,
                  o_ref,
                  h_bf, st_re, st_im, bu_re, bu_im, u_scr, ya_scr):
    @pl.when(pl.program_id(0) == 0)
    def _():
        st_re[...] = jnp.zeros_like(st_re)
        st_im[...] = jnp.zeros_like(st_im)

    def x_slab(j):
        return x_ref[j * SLAB_BATCHES:(j + 1) * SLAB_BATCHES].reshape(SLAB, D_MODEL)

    def slab_rows(j):
        return slice(j * SLAB, (j + 1) * SLAB)

    n_slabs = ROWS // SLAB

    for j in range(n_slabs):
        h = _rms(x_slab(j), gmix_ref[...]).astype(BF16)
        h_bf[slab_rows(j), :] = h
        u = _dot(h, win_ref[:, WIN_U])
        for e in range(SLAB_BATCHES):
            for k in range(SSM_TILES):
                u_scr[k, pl.ds(j * SLAB_BATCHES + e, CHUNK, stride=BATCH), :] = (
                    u[e * CHUNK:(e + 1) * CHUNK, k * LANES:(k + 1) * LANES])

    def sub_rows(j):
        return slice(j * SCAN_ROWS, (j + 1) * SCAN_ROWS)

    def load_u(j):
        return _lane_tiles(u_scr, sub_rows(j), SSM_TILES)

    def project_in(j):
        slot = j % 2
        ub = load_u(j).astype(BF16)
        for k in range(N_BLK):
            bu = _dot(ub[:, k * U_BLK:(k + 1) * U_BLK], bblk_ref[k])
            bu_re[slot, :, k * ST_BLK:(k + 1) * ST_BLK] = bu[:, :ST_BLK]
            bu_im[slot, :, k * ST_BLK:(k + 1) * ST_BLK] = bu[:, ST_BLK:]

    def recur(j, state):
        slot = j % 2
        state = list(state)
        for t in range(SCAN_STEPS):
            rr = slice(t * BATCH, (t + 1) * BATCH)
            for cb in range(STATE_COLS // SCAN_COLS):
                cs = slice(cb * SCAN_COLS, (cb + 1) * SCAN_COLS)
                sre, sim = state[cb]
                ar = are_ref[:, cs]
                ai = aim_ref[:, cs]
                nre = ar * sre - ai * sim + bu_re[slot, rr, cs]
                nim = ar * sim + ai * sre + bu_im[slot, rr, cs]
                bu_re[slot, rr, cs] = nre
                bu_im[slot, rr, cs] = nim
                state[cb] = (nre, nim)
        return state

    def project_out(j):
        slot = j % 2
        ys = []
        for k in range(N_BLK):
            sr = bu_re[slot, :, k * ST_BLK:(k + 1) * ST_BLK].astype(BF16)
            si = bu_im[slot, :, k * ST_BLK:(k + 1) * ST_BLK].astype(BF16)
            ys.append(_dot(sr, cblk_ref[k, :ST_BLK, :]) + _dot(si, cblk_ref[k, ST_BLK:, :]))
        y = jnp.concatenate(ys, axis=1) + dskip_ref[...] * load_u(j)
        y = jax.nn.gelu(y)
        y = y * jax.nn.sigmoid(_dot(y.astype(BF16), wglu_ref[...]) + bglu_ref[...])
        for k in range(SSM_TILES):
            ya_scr[k, sub_rows(j), :] = y[:, k * LANES:(k + 1) * LANES]

    n_sub = CHUNK // SCAN_STEPS
    state = [(st_re[:, cb * SCAN_COLS:(cb + 1) * SCAN_COLS],
              st_im[:, cb * SCAN_COLS:(cb + 1) * SCAN_COLS])
             for cb in range(STATE_COLS // SCAN_COLS)]
    project_in(0)
    for j in range(n_sub):
        if j > 0:
            project_out(j - 1)
        if j + 1 < n_sub:
            project_in(j + 1)
        state = recur(j, state)
    project_out(n_sub - 1)
    for cb, (sre, sim) in enumerate(state):
        st_re[:, cb * SCAN_COLS:(cb + 1) * SCAN_COLS] = sre
        st_im[:, cb * SCAN_COLS:(cb + 1) * SCAN_COLS] = sim

    lane = lax.broadcasted_iota(jnp.int32, (CHUNK, LANES), 1)
    first_group = lane < SGU_GROUP_DIM

    def gate_dots(j):
        hs = h_bf[slab_rows(j), :]
        return (_dot(hs, win_ref[:, WIN_UV]), _dot(hs, win_ref[:, WIN_GB]),
                _dot(hs, win_ref[:, WIN_GA]))

    def merge(j, uv, gb, ga):
        uv = jax.nn.gelu(uv)
        v = _rms(uv[:, SGU_WIDTH:], gsgu_ref[...])
        mixed = []
        for e in range(SLAB_BATCHES):
            tiles = []
            for k in range(SGU_TILES):
                vt = v[e * CHUNK:(e + 1) * CHUNK, k * LANES:(k + 1) * LANES]
                rhs = jnp.concatenate([jnp.where(first_group, vt, 0.0).astype(BF16),
                                       jnp.where(first_group, 0.0, vt).astype(BF16)], axis=0)
                tiles.append(_dot(wsp_ref[k], rhs))
            mixed.append(jnp.concatenate(tiles, axis=1) + bsb_ref[...])
        mixed = jnp.concatenate(mixed, axis=0)
        yb = _dot((uv[:, :SGU_WIDTH] * mixed).astype(BF16), wpb_ref[...])
        ya_in = jnp.concatenate([
            _lane_tiles(ya_scr, pl.ds(j * SLAB_BATCHES + e, CHUNK, stride=BATCH), SSM_TILES)
            for e in range(SLAB_BATCHES)], axis=0)
        ya = _dot(ya_in.astype(BF16), wpa_ref[...])
        m = jax.nn.sigmoid(ga) * ya + jax.nn.sigmoid(gb) * yb
        x1 = x_slab(j) + _dot(m.astype(BF16), wout_ref[...])
        for e in range(SLAB_BATCHES):
            for k in range(D_TILES):
                o_ref[k, pl.ds(j * SLAB_BATCHES + e, CHUNK, stride=BATCH), :] = (
                    x1[e * CHUNK:(e + 1) * CHUNK, k * LANES:(k + 1) * LANES])

    nxt = gate_dots(0)
    for j in range(n_slabs):
        cur = nxt
        if j + 1 < n_slabs:
            nxt = gate_dots(j + 1)
        merge(j, *cur)


def _const_spec(shape):
    nd = len(shape)
    return pl.BlockSpec(shape, lambda i, nd=nd: (0,) * nd, pipeline_mode=pl.Buffered(1))


def _mixer(x, consts):
    return pl.pallas_call(
        _mixer_kernel,
        grid=(N_STEPS,),
        in_specs=[pl.BlockSpec((BATCH, CHUNK, D_MODEL), lambda i: (0, i, 0))]
        + [_const_spec(c.shape) for c in consts],
        out_specs=pl.BlockSpec((D_TILES, ROWS, LANES), lambda i: (0, i, 0)),
        out_shape=jax.ShapeDtypeStruct((D_TILES, SEQ * BATCH, LANES), F32),
        scratch_shapes=[
            pltpu.VMEM((ROWS, D_MODEL), BF16),
            pltpu.VMEM((BATCH, STATE_COLS), F32),
            pltpu.VMEM((BATCH, STATE_COLS), F32),
            pltpu.VMEM((2, SCAN_ROWS, STATE_COLS), F32),
            pltpu.VMEM((2, SCAN_ROWS, STATE_COLS), F32),
            pltpu.VMEM((SSM_TILES, ROWS, LANES), F32),
            pltpu.VMEM((SSM_TILES, ROWS, LANES), F32),
        ],
        compiler_params=pltpu.CompilerParams(
            dimension_semantics=("arbitrary",), vmem_limit_bytes=VMEM_LIMIT),
        name="mixer",
    )(x, *consts)


def _ffn_kernel(x_ref, gffn_ref, wup_ref, cw_ref, cb_ref, wd_ref, gfin_ref, o_ref,
                h_bf, acc, carry, fin_scr):
    @pl.when(pl.program_id(0) == 0)
    def _():
        carry[...] = jnp.zeros_like(carry)

    def slab_rows(j):
        return slice(j * SLAB, (j + 1) * SLAB)

    n_slabs = ROWS // SLAB
    for j in range(n_slabs):
        h_bf[slab_rows(j), :] = _rms(
            _lane_tiles(x_ref, slab_rows(j), D_TILES), gffn_ref[...]).astype(BF16)

    def conv_tile(y, cols):
        prev = carry[:, cols]
        carry[:, cols] = y[ROWS - 2 * BATCH:, :]
        y1 = jnp.concatenate([prev[BATCH:, :], y[:ROWS - BATCH, :]], axis=0)
        y2 = jnp.concatenate([prev, y[:ROWS - 2 * BATCH, :]], axis=0)
        return (y2 * cw_ref[0:1, cols] + y1 * cw_ref[1:2, cols] + y * cw_ref[2:3, cols]
                + cb_ref[:, cols])

    hb = h_bf[...]

    def cols_a(c):
        return slice(FF_BOUNDS[c], FF_BOUNDS[c + 1])

    def cols_b(c):
        return slice(D_FF + FF_BOUNDS[c], D_FF + FF_BOUNDS[c + 1])

    def up(c):
        return _dot(hb, wup_ref[:, cols_a(c)]), _dot(hb, wup_ref[:, cols_b(c)])

    nxt = up(0)
    for c in range(N_FF_TILES):
        ya, yb = nxt
        if c + 1 < N_FF_TILES:
            nxt = up(c + 1)
        a = conv_tile(ya, cols_a(c))
        b = conv_tile(yb, cols_b(c))
        gated = (jax.nn.silu(a) * b).astype(BF16)
        down = _dot(gated, wd_ref[cols_a(c), :])
        if c == 0:
            acc[...] = down
        else:
            acc[...] += down

    for j in range(n_slabs):
        rows = slab_rows(j)
        y = _rms(_lane_tiles(x_ref, rows, D_TILES) + acc[rows, :], gfin_ref[...])
        for k in range(D_TILES):
            fin_scr[k, rows, :] = y[:, k * LANES:(k + 1) * LANES]

    for b in range(BATCH):
        for k in range(D_TILES):
            o_ref[b, :, k * LANES:(k + 1) * LANES] = fin_scr[k, pl.ds(b, CHUNK, stride=BATCH), :]


def _ffn(x1, consts):
    return pl.pallas_call(
        _ffn_kernel,
        grid=(N_STEPS,),
        in_specs=[pl.BlockSpec((D_TILES, ROWS, LANES), lambda i: (0, i, 0))]
        + [_const_spec(c.shape) for c in consts],
        out_specs=pl.BlockSpec((BATCH, CHUNK, D_MODEL), lambda i: (0, i, 0)),
        out_shape=jax.ShapeDtypeStruct((BATCH, SEQ, D_MODEL), F32),
        scratch_shapes=[
            pltpu.VMEM((ROWS, D_MODEL), BF16),
            pltpu.VMEM((ROWS, D_MODEL), F32),
            pltpu.VMEM((2 * BATCH, 2 * D_FF), F32),
            pltpu.VMEM((D_TILES, ROWS, LANES), F32),
        ],
        compiler_params=pltpu.CompilerParams(
            dimension_semantics=("arbitrary",), vmem_limit_bytes=VMEM_LIMIT),
        name="ffn",
    )(x1, *consts)


def _block_diag(blocks):
    g, r, c = blocks.shape
    eye = jnp.eye(g, dtype=blocks.dtype)
    return jnp.einsum("grc,gk->grkc", blocks, eye).reshape(g * r, g * c)


def kernel(x, g_mix, w_in, a_re, a_im, log_dt, b_re, b_im, c_re, c_im, d_skip, w_glu, b_glu, w_proj_a, g_sgu, w_s, b_s, w_proj_b, w_out, g_ffn, w_up, conv_w, conv_b, w_down, g_final):
    l = 0
    abar_re, abar_im, bbar_re, bbar_im, ws_m = _prep(
        a_re[l], a_im[l], log_dt[l][:, None],
        b_re[l].transpose(0, 2, 1), b_im[l].transpose(0, 2, 1), w_s[l])
    are = jnp.broadcast_to(abar_re.reshape(1, STATE_COLS), (BATCH, STATE_COLS))
    aim = jnp.broadcast_to(abar_im.reshape(1, STATE_COLS), (BATCH, STATE_COLS))
    bd_re = _block_diag(bbar_re)
    bd_im = _block_diag(bbar_im)
    bblk = jnp.stack([
        jnp.concatenate([bd_re[k * U_BLK:(k + 1) * U_BLK, k * ST_BLK:(k + 1) * ST_BLK],
                         bd_im[k * U_BLK:(k + 1) * U_BLK, k * ST_BLK:(k + 1) * ST_BLK]], axis=1)
        for k in range(N_BLK)]).astype(BF16)
    cd_re = _block_diag(c_re[l].transpose(0, 2, 1))
    cd_im = _block_diag(c_im[l].transpose(0, 2, 1))
    cblk = jnp.stack([
        jnp.concatenate([cd_re[k * ST_BLK:(k + 1) * ST_BLK, k * U_BLK:(k + 1) * U_BLK],
                         -cd_im[k * ST_BLK:(k + 1) * ST_BLK, k * U_BLK:(k + 1) * U_BLK]], axis=0)
        for k in range(N_BLK)]).astype(BF16)
    wsp = ws_m.reshape(SGU_TILES, 2, CHUNK, CHUNK).transpose(0, 2, 1, 3).reshape(
        SGU_TILES, CHUNK, 2 * CHUNK).astype(BF16)
    bsb = jnp.repeat(b_s[l].T, SGU_GROUP_DIM, axis=1)
    mixer_consts = [
        g_mix[l][None, :], w_in[l].astype(BF16),
        bblk, are, aim, cblk, d_skip[l][None, :], w_glu[l].astype(BF16), b_glu[l][None, :],
        w_proj_a[l].astype(BF16), g_sgu[l][None, :], wsp, bsb,
        w_proj_b[l].astype(BF16), w_out[l].astype(BF16),
    ]
    ffn_consts = [
        g_ffn[l][None, :], w_up[l].astype(BF16), conv_w[l], conv_b[l][None, :],
        w_down[l].astype(BF16), g_final[None, :],
    ]
    x1 = _mixer(x, mixer_consts)
    return _ffn(x1, ffn_consts)
```

```python
import jax
import jax.numpy as jnp
from jax import lax
from jax.experimental import pallas as pl
from jax.experimental.pallas import tpu as pltpu

D_MODEL = 1024
BATCH = 8
SEQ = 4096
SSM_WIDTH = 512
SSM_GROUP = 16
SSM_GROUPS = 32
SSM_STATE = 64
STATE_COLS = SSM_GROUPS * SSM_STATE
SGU_WIDTH = 512
SGU_GROUPS = 8
SGU_GROUP_DIM = 64
CHUNK = 128
LANES = 128
D_TILES = D_MODEL // LANES
SSM_TILES = SSM_WIDTH // LANES
SGU_TILES = SGU_WIDTH // LANES
D_FF = 2816
CONV_WIDTH = 3
EPS = 1e-6

ROWS = CHUNK * BATCH
N_STEPS = SEQ // CHUNK
SLAB = 256
SLAB_BATCHES = SLAB // CHUNK
SCAN_STEPS = 32
SCAN_ROWS = SCAN_STEPS * BATCH
SCAN_COLS = 1024
U_BLK = 128
GPB = U_BLK // SSM_GROUP
ST_BLK = GPB * SSM_STATE
N_BLK = SSM_WIDTH // U_BLK
WIN_U = slice(0, SSM_WIDTH)
WIN_UV = slice(SSM_WIDTH, SSM_WIDTH + 2 * SGU_WIDTH)
WIN_GA = slice(SSM_WIDTH + 2 * SGU_WIDTH, SSM_WIDTH + 2 * SGU_WIDTH + D_MODEL)
WIN_GB = slice(SSM_WIDTH + 2 * SGU_WIDTH + D_MODEL, SSM_WIDTH + 2 * SGU_WIDTH + 2 * D_MODEL)
FF_TILE = 512
FF_BOUNDS = list(range(0, D_FF, FF_TILE)) + [D_FF]
N_FF_TILES = len(FF_BOUNDS) - 1
VMEM_LIMIT = 60 * 1024 * 1024

F32 = jnp.float32
BF16 = jnp.bfloat16


def _dot(a, b):
    return jnp.dot(a, b, preferred_element_type=F32)


def _rms(x, g):
    return x * lax.rsqrt(jnp.mean(x * x, axis=-1, keepdims=True) + EPS) * g


def _lane_tiles(ref, rows, n):
    return jnp.concatenate([ref[k, rows, :] for k in range(n)], axis=1)


def _prep_kernel(are_ref, aim_ref, ldt_ref, bre_ref, bim_ref, ws_ref,
                 abre_ref, abim_ref, bbre_ref, bbim_ref, wsm_ref):
    dt = jnp.exp(ldt_ref[...])
    ar = are_ref[...]
    ai = aim_ref[...]
    mag = jnp.exp(dt * ar)
    abar_re = mag * jnp.cos(dt * ai)
    abar_im = mag * jnp.sin(dt * ai)
    den = ar * ar + ai * ai
    nr = abar_re - 1.0
    ni = abar_im
    f_re = (nr * ar + ni * ai) / den
    f_im = (ni * ar - nr * ai) / den
    abre_ref[...] = abar_re
    abim_ref[...] = abar_im
    br = bre_ref[...]
    bi = bim_ref[...]
    fr = f_re[:, None, :]
    fi = f_im[:, None, :]
    bbre_ref[...] = fr * br - fi * bi
    bbim_ref[...] = fr * bi + fi * br
    row = lax.broadcasted_iota(jnp.int32, (SGU_GROUPS, CHUNK, CHUNK), 1)
    col = lax.broadcasted_iota(jnp.int32, (SGU_GROUPS, CHUNK, CHUNK), 2)
    wsm_ref[...] = jnp.where(row >= col, ws_ref[...], 0.0)


def _prep(a_re, a_im, log_dt, b_re_t, b_im_t, w_s):
    g, p, h = SSM_GROUPS, SSM_STATE, SSM_GROUP
    return pl.pallas_call(
        _prep_kernel,
        out_shape=(
            jax.ShapeDtypeStruct((g, p), F32),
            jax.ShapeDtypeStruct((g, p), F32),
            jax.ShapeDtypeStruct((g, h, p), F32),
            jax.ShapeDtypeStruct((g, h, p), F32),
            jax.ShapeDtypeStruct((SGU_GROUPS, CHUNK, CHUNK), F32),
        ),
        name="prep",
    )(a_re, a_im, log_dt, b_re_t, b_im_t, w_s)


def _mixer_kernel(x_ref, gmix_ref, win_ref,
                  bblk_ref, are_ref, aim_ref, cblk_ref, dskip_ref, wglu_ref, bglu_ref,
                  wpa_ref, gsgu_ref, wsp_ref, bsb_ref, wpb_ref, wout_ref,
                  o_ref,
                  h_bf, st_re, st_im, bu_re, bu_im, u_scr, ya_scr):
    @pl.when(pl.program_id(0) == 0)
    def _():
        st_re[...] = jnp.zeros_like(st_re)
        st_im[...] = jnp.zeros_like(st_im)

    def x_slab(j):
        return x_ref[j * SLAB_BATCHES:(j + 1) * SLAB_BATCHES].reshape(SLAB, D_MODEL)

    def slab_rows(j):
        return slice(j * SLAB, (j + 1) * SLAB)

    n_slabs = ROWS // SLAB

    for j in range(n_slabs):
        h = _rms(x_slab(j), gmix_ref[...]).astype(BF16)
        h_bf[slab_rows(j), :] = h
        u = _dot(h, win_ref[:, WIN_U])
        for e in range(SLAB_BATCHES):
            for k in range(SSM_TILES):
                u_scr[k, pl.ds(j * SLAB_BATCHES + e, CHUNK, stride=BATCH), :] = (
                    u[e * CHUNK:(e + 1) * CHUNK, k * LANES:(k + 1) * LANES])

    def sub_rows(j):
        return slice(j * SCAN_ROWS, (j + 1) * SCAN_ROWS)

    def load_u(j):
        return _lane_tiles(u_scr, sub_rows(j), SSM_TILES)

    def project_in(j):
        slot = j % 2
        ub = load_u(j).astype(BF16)
        for k in range(N_BLK):
            bu = _dot(ub[:, k * U_BLK:(k + 1) * U_BLK], bblk_ref[k])
            bu_re[slot, :, k * ST_BLK:(k + 1) * ST_BLK] = bu[:, :ST_BLK]
            bu_im[slot, :, k * ST_BLK:(k + 1) * ST_BLK] = bu[:, ST_BLK:]

    def recur(j, state):
        slot = j % 2
        state = list(state)
        for t in range(SCAN_STEPS):
            rr = slice(t * BATCH, (t + 1) * BATCH)
            for cb in range(STATE_COLS // SCAN_COLS):
                cs = slice(cb * SCAN_COLS, (cb + 1) * SCAN_COLS)
                sre, sim = state[cb]
                ar = are_ref[:, cs]
                ai = aim_ref[:, cs]
                nre = ar * sre - ai * sim + bu_re[slot, rr, cs]
                nim = ar * sim + ai * sre + bu_im[slot, rr, cs]
                bu_re[slot, rr, cs] = nre
                bu_im[slot, rr, cs] = nim
                state[cb] = (nre, nim)
        return state

    def project_out(j):
        slot = j % 2
        ys = []
        for k in range(N_BLK):
            sr = bu_re[slot, :, k * ST_BLK:(k + 1) * ST_BLK].astype(BF16)
            si = bu_im[slot, :, k * ST_BLK:(k + 1) * ST_BLK].astype(BF16)
            ys.append(_dot(sr, cblk_ref[k, :ST_BLK, :]) + _dot(si, cblk_ref[k, ST_BLK:, :]))
        y = jnp.concatenate(ys, axis=1) + dskip_ref[...] * load_u(j)
        y = jax.nn.gelu(y)
        y = y * jax.nn.sigmoid(_dot(y.astype(BF16), wglu_ref[...]) + bglu_ref[...])
        for k in range(SSM_TILES):
            ya_scr[k, sub_rows(j), :] = y[:, k * LANES:(k + 1) * LANES]

    n_sub = CHUNK // SCAN_STEPS
    state = [(st_re[:, cb * SCAN_COLS:(cb + 1) * SCAN_COLS],
              st_im[:, cb * SCAN_COLS:(cb + 1) * SCAN_COLS])
             for cb in range(STATE_COLS // SCAN_COLS)]
    def gate_dots(j):
        hs = h_bf[slab_rows(j), :]
        return (_dot(hs, win_ref[:, WIN_UV]), _dot(hs, win_ref[:, WIN_GB]),
                _dot(hs, win_ref[:, WIN_GA]))

    project_in(0)
    for j in range(n_sub):
        if j > 0:
            project_out(j - 1)
        if j + 1 < n_sub:
            project_in(j + 1)
        state = recur(j, state)
    project_out(n_sub - 1)
    for cb, (sre, sim) in enumerate(state):
        st_re[:, cb * SCAN_COLS:(cb + 1) * SCAN_COLS] = sre
        st_im[:, cb * SCAN_COLS:(cb + 1) * SCAN_COLS] = sim

    lane = lax.broadcasted_iota(jnp.int32, (CHUNK, LANES), 1)
    first_group = lane < SGU_GROUP_DIM

    def merge(j, uv, gb, ga):
        uv = jax.nn.gelu(uv)
        v = _rms(uv[:, SGU_WIDTH:], gsgu_ref[...])
        mixed = []
        for e in range(SLAB_BATCHES):
            tiles = []
            for k in range(SGU_TILES):
                vt = v[e * CHUNK:(e + 1) * CHUNK, k * LANES:(k + 1) * LANES]
                rhs = jnp.concatenate([jnp.where(first_group, vt, 0.0).astype(BF16),
                                       jnp.where(first_group, 0.0, vt).astype(BF16)], axis=0)
                tiles.append(_dot(wsp_ref[k], rhs))
            mixed.append(jnp.concatenate(tiles, axis=1) + bsb_ref[...])
        mixed = jnp.concatenate(mixed, axis=0)
        yb = _dot((uv[:, :SGU_WIDTH] * mixed).astype(BF16), wpb_ref[...])
        ya_in = jnp.concatenate([
            _lane_tiles(ya_scr, pl.ds(j * SLAB_BATCHES + e, CHUNK, stride=BATCH), SSM_TILES)
            for e in range(SLAB_BATCHES)], axis=0)
        ya = _dot(ya_in.astype(BF16), wpa_ref[...])
        m = jax.nn.sigmoid(ga) * ya + jax.nn.sigmoid(gb) * yb
        x1 = x_slab(j) + _dot(m.astype(BF16), wout_ref[...])
        for e in range(SLAB_BATCHES):
            for k in range(D_TILES):
                o_ref[k, pl.ds(j * SLAB_BATCHES + e, CHUNK, stride=BATCH), :] = (
                    x1[e * CHUNK:(e + 1) * CHUNK, k * LANES:(k + 1) * LANES])

    nxt = gate_dots(0)
    for j in range(n_slabs):
        cur = nxt
        if j + 1 < n_slabs:
            nxt = gate_dots(j + 1)
        merge(j, *cur)


def _const_spec(shape):
    nd = len(shape)
    return pl.BlockSpec(shape, lambda i, nd=nd: (0,) * nd, pipeline_mode=pl.Buffered(1))


def _mixer(x, consts):
    return pl.pallas_call(
        _mixer_kernel,
        grid=(N_STEPS,),
        in_specs=[pl.BlockSpec((BATCH, CHUNK, D_MODEL), lambda i: (0, i, 0))]
        + [_const_spec(c.shape) for c in consts],
        out_specs=pl.BlockSpec((D_TILES, ROWS, LANES), lambda i: (0, i, 0)),
        out_shape=jax.ShapeDtypeStruct((D_TILES, SEQ * BATCH, LANES), F32),
        scratch_shapes=[
            pltpu.VMEM((ROWS, D_MODEL), BF16),
            pltpu.VMEM((BATCH, STATE_COLS), F32),
            pltpu.VMEM((BATCH, STATE_COLS), F32),
            pltpu.VMEM((2, SCAN_ROWS, STATE_COLS), F32),
            pltpu.VMEM((2, SCAN_ROWS, STATE_COLS), F32),
            pltpu.VMEM((SSM_TILES, ROWS, LANES), F32),
            pltpu.VMEM((SSM_TILES, ROWS, LANES), F32),
        ],
        compiler_params=pltpu.CompilerParams(
            dimension_semantics=("arbitrary",), vmem_limit_bytes=VMEM_LIMIT),
        name="mixer",
    )(x, *consts)


def _ffn_kernel(x_ref, gffn_ref, wup_ref, cw_ref, cb_ref, wd_ref, gfin_ref, o_ref,
                h_bf, acc, carry, fin_scr):
    @pl.when(pl.program_id(0) == 0)
    def _():
        carry[...] = jnp.zeros_like(carry)

    def slab_rows(j):
        return slice(j * SLAB, (j + 1) * SLAB)

    n_slabs = ROWS // SLAB
    for j in range(n_slabs):
        h_bf[slab_rows(j), :] = _rms(
            _lane_tiles(x_ref, slab_rows(j), D_TILES), gffn_ref[...]).astype(BF16)

    def conv_tile(y, cols):
        prev = carry[:, cols]
        carry[:, cols] = y[ROWS - 2 * BATCH:, :]
        y1 = jnp.concatenate([prev[BATCH:, :], y[:ROWS - BATCH, :]], axis=0)
        y2 = jnp.concatenate([prev, y[:ROWS - 2 * BATCH, :]], axis=0)
        return (y2 * cw_ref[0:1, cols] + y1 * cw_ref[1:2, cols] + y * cw_ref[2:3, cols]
                + cb_ref[:, cols])

    hb = h_bf[...]

    def cols_a(c):
        return slice(FF_BOUNDS[c], FF_BOUNDS[c + 1])

    def cols_b(c):
        return slice(D_FF + FF_BOUNDS[c], D_FF + FF_BOUNDS[c + 1])

    def up(c):
        return _dot(hb, wup_ref[:, cols_a(c)]), _dot(hb, wup_ref[:, cols_b(c)])

    nxt = up(0)
    for c in range(N_FF_TILES):
        ya, yb = nxt
        if c + 1 < N_FF_TILES:
            nxt = up(c + 1)
        a = conv_tile(ya, cols_a(c))
        b = conv_tile(yb, cols_b(c))
        gated = (jax.nn.silu(a) * b).astype(BF16)
        down = _dot(gated, wd_ref[cols_a(c), :])
        if c == 0:
            acc[...] = down
        else:
            acc[...] += down

    for j in range(n_slabs):
        rows = slab_rows(j)
        y = _rms(_lane_tiles(x_ref, rows, D_TILES) + acc[rows, :], gfin_ref[...])
        for k in range(D_TILES):
            fin_scr[k, rows, :] = y[:, k * LANES:(k + 1) * LANES]

    for b in range(BATCH):
        for k in range(D_TILES):
            o_ref[b, :, k * LANES:(k + 1) * LANES] = fin_scr[k, pl.ds(b, CHUNK, stride=BATCH), :]


def _ffn(x1, consts):
    return pl.pallas_call(
        _ffn_kernel,
        grid=(N_STEPS,),
        in_specs=[pl.BlockSpec((D_TILES, ROWS, LANES), lambda i: (0, i, 0))]
        + [_const_spec(c.shape) for c in consts],
        out_specs=pl.BlockSpec((BATCH, CHUNK, D_MODEL), lambda i: (0, i, 0)),
        out_shape=jax.ShapeDtypeStruct((BATCH, SEQ, D_MODEL), F32),
        scratch_shapes=[
            pltpu.VMEM((ROWS, D_MODEL), BF16),
            pltpu.VMEM((ROWS, D_MODEL), F32),
            pltpu.VMEM((2 * BATCH, 2 * D_FF), F32),
            pltpu.VMEM((D_TILES, ROWS, LANES), F32),
        ],
        compiler_params=pltpu.CompilerParams(
            dimension_semantics=("arbitrary",), vmem_limit_bytes=VMEM_LIMIT),
        name="ffn",
    )(x1, *consts)


def kernel(x, g_mix, w_in, a_re, a_im, log_dt, b_re, b_im, c_re, c_im, d_skip, w_glu, b_glu, w_proj_a, g_sgu, w_s, b_s, w_proj_b, w_out, g_ffn, w_up, conv_w, conv_b, w_down, g_final):
    l = 0
    abar_re, abar_im, bbar_re, bbar_im, ws_m = _prep(
        a_re[l], a_im[l], log_dt[l][:, None],
        b_re[l].transpose(0, 2, 1), b_im[l].transpose(0, 2, 1), w_s[l])
    are = jnp.broadcast_to(abar_re.reshape(1, STATE_COLS), (BATCH, STATE_COLS))
    aim = jnp.broadcast_to(abar_im.reshape(1, STATE_COLS), (BATCH, STATE_COLS))
    split = lambda re, im: jnp.stack([re, im]).reshape(2, N_BLK, GPB, SSM_GROUP, SSM_STATE)
    eye = jnp.eye(GPB, dtype=F32)
    bblk = jnp.einsum("rkghp,gq->kghrqp", split(bbar_re, bbar_im), eye).reshape(
        N_BLK, U_BLK, 2 * ST_BLK).astype(BF16)
    cblk = jnp.einsum("rkghp,gq->krgpqh", split(c_re[l], -c_im[l]), eye).reshape(
        N_BLK, 2 * ST_BLK, U_BLK).astype(BF16)
    wsp = ws_m.reshape(SGU_TILES, 2, CHUNK, CHUNK).transpose(0, 2, 1, 3).reshape(
        SGU_TILES, CHUNK, 2 * CHUNK).astype(BF16)
    bsb = jnp.repeat(b_s[l].T, SGU_GROUP_DIM, axis=1)
    mixer_consts = [
        g_mix[l][None, :], w_in[l].astype(BF16),
        bblk, are, aim, cblk, d_skip[l][None, :], w_glu[l].astype(BF16), b_glu[l][None, :],
        w_proj_a[l].astype(BF16), g_sgu[l][None, :], wsp, bsb,
        w_proj_b[l].astype(BF16), w_out[l].astype(BF16),
    ]
    ffn_consts = [
        g_ffn[l][None, :], w_up[l].astype(BF16), conv_w[l], conv_b[l][None, :],
        w_down[l].astype(BF16), g_final[None, :],
    ]
    x1 = _mixer(x, mixer_consts)
    return _ffn(x1, ffn_consts)
```

```python
import jax
import jax.numpy as jnp
from jax import lax
from jax.experimental import pallas as pl
from jax.experimental.pallas import tpu as pltpu

D_MODEL = 1024
BATCH = 8
SEQ = 4096
SSM_WIDTH = 512
SSM_GROUP = 16
SSM_GROUPS = 32
SSM_STATE = 64
STATE_COLS = SSM_GROUPS * SSM_STATE
SGU_WIDTH = 512
SGU_GROUPS = 8
SGU_GROUP_DIM = 64
CHUNK = 128
LANES = 128
D_TILES = D_MODEL // LANES
SSM_TILES = SSM_WIDTH // LANES
SGU_TILES = SGU_WIDTH // LANES
D_FF = 2816
CONV_WIDTH = 3
EPS = 1e-6

ROWS = CHUNK * BATCH
N_STEPS = SEQ // CHUNK
SLAB = 512
SLAB_BATCHES = SLAB // CHUNK
FFN_SLAB = 256
SCAN_STEPS = 32
SCAN_ROWS = SCAN_STEPS * BATCH
SCAN_COLS = 1024
U_BLK = 128
GPB = U_BLK // SSM_GROUP
ST_BLK = GPB * SSM_STATE
N_BLK = SSM_WIDTH // U_BLK
WIN_U = slice(0, SSM_WIDTH)
WIN_UV = slice(SSM_WIDTH, SSM_WIDTH + 2 * SGU_WIDTH)
WIN_GA = slice(SSM_WIDTH + 2 * SGU_WIDTH, SSM_WIDTH + 2 * SGU_WIDTH + D_MODEL)
WIN_GB = slice(SSM_WIDTH + 2 * SGU_WIDTH + D_MODEL, SSM_WIDTH + 2 * SGU_WIDTH + 2 * D_MODEL)
FF_TILE = 512
FF_BOUNDS = list(range(0, D_FF, FF_TILE)) + [D_FF]
N_FF_TILES = len(FF_BOUNDS) - 1
VMEM_LIMIT = 60 * 1024 * 1024

F32 = jnp.float32
BF16 = jnp.bfloat16


def _dot(a, b):
    return jnp.dot(a, b, preferred_element_type=F32)


def _rms(x, g):
    return x * lax.rsqrt(jnp.mean(x * x, axis=-1, keepdims=True) + EPS) * g


def _lane_tiles(ref, rows, n):
    return jnp.concatenate([ref[k, rows, :] for k in range(n)], axis=1)


def _prep_kernel(are_ref, aim_ref, ldt_ref, bre_ref, bim_ref, ws_ref,
                 abre_ref, abim_ref, bbre_ref, bbim_ref, wsm_ref):
    dt = jnp.exp(ldt_ref[...])
    ar = are_ref[...]
    ai = aim_ref[...]
    mag = jnp.exp(dt * ar)
    abar_re = mag * jnp.cos(dt * ai)
    abar_im = mag * jnp.sin(dt * ai)
    den = ar * ar + ai * ai
    nr = abar_re - 1.0
    ni = abar_im
    f_re = (nr * ar + ni * ai) / den
    f_im = (ni * ar - nr * ai) / den
    abre_ref[...] = abar_re
    abim_ref[...] = abar_im
    br = bre_ref[...]
    bi = bim_ref[...]
    fr = f_re[:, None, :]
    fi = f_im[:, None, :]
    bbre_ref[...] = fr * br - fi * bi
    bbim_ref[...] = fr * bi + fi * br
    row = lax.broadcasted_iota(jnp.int32, (SGU_GROUPS, CHUNK, CHUNK), 1)
    col = lax.broadcasted_iota(jnp.int32, (SGU_GROUPS, CHUNK, CHUNK), 2)
    wsm_ref[...] = jnp.where(row >= col, ws_ref[...], 0.0)


def _prep(a_re, a_im, log_dt, b_re_t, b_im_t, w_s):
    g, p, h = SSM_GROUPS, SSM_STATE, SSM_GROUP
    return pl.pallas_call(
        _prep_kernel,
        out_shape=(
            jax.ShapeDtypeStruct((g, p), F32),
            jax.ShapeDtypeStruct((g, p), F32),
            jax.ShapeDtypeStruct((g, h, p), F32),
            jax.ShapeDtypeStruct((g, h, p), F32),
            jax.ShapeDtypeStruct((SGU_GROUPS, CHUNK, CHUNK), F32),
        ),
        name="prep",
    )(a_re, a_im, log_dt, b_re_t, b_im_t, w_s)


def _mixer_kernel(x_ref, gmix_ref, win_ref,
                  bblk_ref, are_ref, aim_ref, cblk_ref, dskip_ref, wglu_ref, bglu_ref,
                  wpa_ref, gsgu_ref, wsp_ref, bsb_ref, wpb_ref, wout_ref,
                  o_ref,
                  h_bf, st_re, st_im, bu_re, bu_im, u_scr, ya_scr):
    @pl.when(pl.program_id(0) == 0)
    def _():
        st_re[...] = jnp.zeros_like(st_re)
        st_im[...] = jnp.zeros_like(st_im)

    def x_slab(j):
        return x_ref[j * SLAB_BATCHES:(j + 1) * SLAB_BATCHES].reshape(SLAB, D_MODEL)

    def slab_rows(j):
        return slice(j * SLAB, (j + 1) * SLAB)

    n_slabs = ROWS // SLAB

    for j in range(n_slabs):
        h = _rms(x_slab(j), gmix_ref[...]).astype(BF16)
        h_bf[slab_rows(j), :] = h
        u = _dot(h, win_ref[:, WIN_U])
        for e in range(SLAB_BATCHES):
            for k in range(SSM_TILES):
                u_scr[k, pl.ds(j * SLAB_BATCHES + e, CHUNK, stride=BATCH), :] = (
                    u[e * CHUNK:(e + 1) * CHUNK, k * LANES:(k + 1) * LANES])

    def sub_rows(j):
        return slice(j * SCAN_ROWS, (j + 1) * SCAN_ROWS)

    def load_u(j):
        return _lane_tiles(u_scr, sub_rows(j), SSM_TILES)

    def project_in(j):
        slot = j % 2
        ub = load_u(j).astype(BF16)
        for k in range(N_BLK):
            bu = _dot(ub[:, k * U_BLK:(k + 1) * U_BLK], bblk_ref[k])
            bu_re[slot, :, k * ST_BLK:(k + 1) * ST_BLK] = bu[:, :ST_BLK]
            bu_im[slot, :, k * ST_BLK:(k + 1) * ST_BLK] = bu[:, ST_BLK:]

    def recur(j, state):
        slot = j % 2
        state = list(state)
        for t in range(SCAN_STEPS):
            rr = slice(t * BATCH, (t + 1) * BATCH)
            for cb in range(STATE_COLS // SCAN_COLS):
                cs = slice(cb * SCAN_COLS, (cb + 1) * SCAN_COLS)
                sre, sim = state[cb]
                ar = are_ref[:, cs]
                ai = aim_ref[:, cs]
                nre = ar * sre - ai * sim + bu_re[slot, rr, cs]
                nim = ar * sim + ai * sre + bu_im[slot, rr, cs]
                bu_re[slot, rr, cs] = nre
                bu_im[slot, rr, cs] = nim
                state[cb] = (nre, nim)
        return state

    def project_out(j):
        slot = j % 2
        ys = []
        for k in range(N_BLK):
            sr = bu_re[slot, :, k * ST_BLK:(k + 1) * ST_BLK].astype(BF16)
            si = bu_im[slot, :, k * ST_BLK:(k + 1) * ST_BLK].astype(BF16)
            ys.append(_dot(sr, cblk_ref[k, :ST_BLK, :]) + _dot(si, cblk_ref[k, ST_BLK:, :]))
        y = jnp.concatenate(ys, axis=1) + dskip_ref[...] * load_u(j)
        y = jax.nn.gelu(y)
        y = y * jax.nn.sigmoid(_dot(y.astype(BF16), wglu_ref[...]) + bglu_ref[...])
        for k in range(SSM_TILES):
            ya_scr[k, sub_rows(j), :] = y[:, k * LANES:(k + 1) * LANES]

    n_sub = CHUNK // SCAN_STEPS
    state = [(st_re[:, cb * SCAN_COLS:(cb + 1) * SCAN_COLS],
              st_im[:, cb * SCAN_COLS:(cb + 1) * SCAN_COLS])
             for cb in range(STATE_COLS // SCAN_COLS)]
    def gate_dots(j):
        hs = h_bf[slab_rows(j), :]
        return (_dot(hs, win_ref[:, WIN_UV]), _dot(hs, win_ref[:, WIN_GB]),
                _dot(hs, win_ref[:, WIN_GA]))

    project_in(0)
    for j in range(n_sub):
        if j > 0:
            project_out(j - 1)
        if j + 1 < n_sub:
            project_in(j + 1)
        state = recur(j, state)
    project_out(n_sub - 1)
    for cb, (sre, sim) in enumerate(state):
        st_re[:, cb * SCAN_COLS:(cb + 1) * SCAN_COLS] = sre
        st_im[:, cb * SCAN_COLS:(cb + 1) * SCAN_COLS] = sim

    lane = lax.broadcasted_iota(jnp.int32, (CHUNK, LANES), 1)
    first_group = lane < SGU_GROUP_DIM

    def merge(j, uv, gb, ga):
        uv = jax.nn.gelu(uv)
        v = _rms(uv[:, SGU_WIDTH:], gsgu_ref[...])
        mixed = []
        for e in range(SLAB_BATCHES):
            tiles = []
            for k in range(SGU_TILES):
                vt = v[e * CHUNK:(e + 1) * CHUNK, k * LANES:(k + 1) * LANES]
                rhs = jnp.concatenate([jnp.where(first_group, vt, 0.0).astype(BF16),
                                       jnp.where(first_group, 0.0, vt).astype(BF16)], axis=0)
                tiles.append(_dot(wsp_ref[k], rhs))
            mixed.append(jnp.concatenate(tiles, axis=1) + bsb_ref[...])
        mixed = jnp.concatenate(mixed, axis=0)
        yb = _dot((uv[:, :SGU_WIDTH] * mixed).astype(BF16), wpb_ref[...])
        ya_in = jnp.concatenate([
            _lane_tiles(ya_scr, pl.ds(j * SLAB_BATCHES + e, CHUNK, stride=BATCH), SSM_TILES)
            for e in range(SLAB_BATCHES)], axis=0)
        ya = _dot(ya_in.astype(BF16), wpa_ref[...])
        m = jax.nn.sigmoid(ga) * ya + jax.nn.sigmoid(gb) * yb
        x1 = x_slab(j) + _dot(m.astype(BF16), wout_ref[...])
        for e in range(SLAB_BATCHES):
            for k in range(D_TILES):
                o_ref[k, pl.ds(j * SLAB_BATCHES + e, CHUNK, stride=BATCH), :] = (
                    x1[e * CHUNK:(e + 1) * CHUNK, k * LANES:(k + 1) * LANES])

    nxt = gate_dots(0)
    for j in range(n_slabs):
        cur = nxt
        if j + 1 < n_slabs:
            nxt = gate_dots(j + 1)
        merge(j, *cur)


def _const_spec(shape):
    nd = len(shape)
    return pl.BlockSpec(shape, lambda i, nd=nd: (0,) * nd, pipeline_mode=pl.Buffered(1))


def _mixer(x, consts):
    return pl.pallas_call(
        _mixer_kernel,
        grid=(N_STEPS,),
        in_specs=[pl.BlockSpec((BATCH, CHUNK, D_MODEL), lambda i: (0, i, 0))]
        + [_const_spec(c.shape) for c in consts],
        out_specs=pl.BlockSpec((D_TILES, ROWS, LANES), lambda i: (0, i, 0)),
        out_shape=jax.ShapeDtypeStruct((D_TILES, SEQ * BATCH, LANES), F32),
        scratch_shapes=[
            pltpu.VMEM((ROWS, D_MODEL), BF16),
            pltpu.VMEM((BATCH, STATE_COLS), F32),
            pltpu.VMEM((BATCH, STATE_COLS), F32),
            pltpu.VMEM((2, SCAN_ROWS, STATE_COLS), F32),
            pltpu.VMEM((2, SCAN_ROWS, STATE_COLS), F32),
            pltpu.VMEM((SSM_TILES, ROWS, LANES), F32),
            pltpu.VMEM((SSM_TILES, ROWS, LANES), F32),
        ],
        compiler_params=pltpu.CompilerParams(
            dimension_semantics=("arbitrary",), vmem_limit_bytes=VMEM_LIMIT),
        name="mixer",
    )(x, *consts)


def _ffn_kernel(x_ref, gffn_ref, wup_ref, cw_ref, cb_ref, wd_ref, gfin_ref, o_ref,
                h_bf, acc, carry, fin_scr):
    @pl.when(pl.program_id(0) == 0)
    def _():
        carry[...] = jnp.zeros_like(carry)

    def slab_rows(j):
        return slice(j * FFN_SLAB, (j + 1) * FFN_SLAB)

    n_slabs = ROWS // FFN_SLAB
    for j in range(n_slabs):
        h_bf[slab_rows(j), :] = _rms(
            _lane_tiles(x_ref, slab_rows(j), D_TILES), gffn_ref[...]).astype(BF16)

    def conv_tile(y, cols):
        prev = carry[:, cols]
        carry[:, cols] = y[ROWS - 2 * BATCH:, :]
        y1 = jnp.concatenate([prev[BATCH:, :], y[:ROWS - BATCH, :]], axis=0)
        y2 = jnp.concatenate([prev, y[:ROWS - 2 * BATCH, :]], axis=0)
        return (y2 * cw_ref[0:1, cols] + y1 * cw_ref[1:2, cols] + y * cw_ref[2:3, cols]
                + cb_ref[:, cols])

    hb = h_bf[...]

    def cols_a(c):
        return slice(FF_BOUNDS[c], FF_BOUNDS[c + 1])

    def cols_b(c):
        return slice(D_FF + FF_BOUNDS[c], D_FF + FF_BOUNDS[c + 1])

    def up(c):
        return _dot(hb, wup_ref[:, cols_a(c)]), _dot(hb, wup_ref[:, cols_b(c)])

    nxt = up(0)
    for c in range(N_FF_TILES):
        ya, yb = nxt
        if c + 1 < N_FF_TILES:
            nxt = up(c + 1)
        a = conv_tile(ya, cols_a(c))
        b = conv_tile(yb, cols_b(c))
        gated = (jax.nn.silu(a) * b).astype(BF16)
        down = _dot(gated, wd_ref[cols_a(c), :])
        if c == 0:
            acc[...] = down
        else:
            acc[...] += down

    for j in range(n_slabs):
        rows = slab_rows(j)
        y = _rms(_lane_tiles(x_ref, rows, D_TILES) + acc[rows, :], gfin_ref[...])
        for k in range(D_TILES):
            fin_scr[k, rows, :] = y[:, k * LANES:(k + 1) * LANES]

    for b in range(BATCH):
        for k in range(D_TILES):
            o_ref[b, :, k * LANES:(k + 1) * LANES] = fin_scr[k, pl.ds(b, CHUNK, stride=BATCH), :]


def _ffn(x1, consts):
    return pl.pallas_call(
        _ffn_kernel,
        grid=(N_STEPS,),
        in_specs=[pl.BlockSpec((D_TILES, ROWS, LANES), lambda i: (0, i, 0))]
        + [_const_spec(c.shape) for c in consts],
        out_specs=pl.BlockSpec((BATCH, CHUNK, D_MODEL), lambda i: (0, i, 0)),
        out_shape=jax.ShapeDtypeStruct((BATCH, SEQ, D_MODEL), F32),
        scratch_shapes=[
            pltpu.VMEM((ROWS, D_MODEL), BF16),
            pltpu.VMEM((ROWS, D_MODEL), F32),
            pltpu.VMEM((2 * BATCH, 2 * D_FF), F32),
            pltpu.VMEM((D_TILES, ROWS, LANES), F32),
        ],
        compiler_params=pltpu.CompilerParams(
            dimension_semantics=("arbitrary",), vmem_limit_bytes=VMEM_LIMIT),
        name="ffn",
    )(x1, *consts)


def kernel(x, g_mix, w_in, a_re, a_im, log_dt, b_re, b_im, c_re, c_im, d_skip, w_glu, b_glu, w_proj_a, g_sgu, w_s, b_s, w_proj_b, w_out, g_ffn, w_up, conv_w, conv_b, w_down, g_final):
    l = 0
    abar_re, abar_im, bbar_re, bbar_im, ws_m = _prep(
        a_re[l], a_im[l], log_dt[l][:, None],
        b_re[l].transpose(0, 2, 1), b_im[l].transpose(0, 2, 1), w_s[l])
    are = jnp.broadcast_to(abar_re.reshape(1, STATE_COLS), (BATCH, STATE_COLS))
    aim = jnp.broadcast_to(abar_im.reshape(1, STATE_COLS), (BATCH, STATE_COLS))
    split = lambda re, im: jnp.stack([re, im]).reshape(2, N_BLK, GPB, SSM_GROUP, SSM_STATE)
    eye = jnp.eye(GPB, dtype=F32)
    bblk = jnp.einsum("rkghp,gq->kghrqp", split(bbar_re, bbar_im), eye).reshape(
        N_BLK, U_BLK, 2 * ST_BLK).astype(BF16)
    cblk = jnp.einsum("rkghp,gq->krgpqh", split(c_re[l], -c_im[l]), eye).reshape(
        N_BLK, 2 * ST_BLK, U_BLK).astype(BF16)
    wsp = ws_m.reshape(SGU_TILES, 2, CHUNK, CHUNK).transpose(0, 2, 1, 3).reshape(
        SGU_TILES, CHUNK, 2 * CHUNK).astype(BF16)
    bsb = jnp.repeat(b_s[l].T, SGU_GROUP_DIM, axis=1)
    mixer_consts = [
        g_mix[l][None, :], w_in[l].astype(BF16),
        bblk, are, aim, cblk, d_skip[l][None, :], w_glu[l].astype(BF16), b_glu[l][None, :],
        w_proj_a[l].astype(BF16), g_sgu[l][None, :], wsp, bsb,
        w_proj_b[l].astype(BF16), w_out[l].astype(BF16),
    ]
    ffn_consts = [
        g_ffn[l][None, :], w_up[l].astype(BF16), conv_w[l], conv_b[l][None, :],
        w_down[l].astype(BF16), g_final[None, :],
    ]
    x1 = _mixer(x, mixer_consts)
    return _ffn(x1, ffn_consts)
```

```python
import jax
import jax.numpy as jnp
from jax import lax
from jax.experimental import pallas as pl
from jax.experimental.pallas import tpu as pltpu

D_MODEL = 1024
BATCH = 8
SEQ = 4096
SSM_WIDTH = 512
SSM_GROUP = 16
SSM_GROUPS = 32
SSM_STATE = 64
STATE_COLS = SSM_GROUPS * SSM_STATE
SGU_WIDTH = 512
SGU_GROUPS = 8
SGU_GROUP_DIM = 64
CHUNK = 128
LANES = 128
D_TILES = D_MODEL // LANES
SSM_TILES = SSM_WIDTH // LANES
SGU_TILES = SGU_WIDTH // LANES
D_FF = 2816
CONV_WIDTH = 3
EPS = 1e-6

ROWS = CHUNK * BATCH
N_STEPS = SEQ // CHUNK
SLAB = 512
SLAB_BATCHES = SLAB // CHUNK
FFN_SLAB = 256
SCAN_STEPS = 32
SCAN_ROWS = SCAN_STEPS * BATCH
SCAN_COLS = 1024
U_BLK = 128
GPB = U_BLK // SSM_GROUP
ST_BLK = GPB * SSM_STATE
N_BLK = SSM_WIDTH // U_BLK
WIN_U = slice(0, SSM_WIDTH)
WIN_UV = slice(SSM_WIDTH, SSM_WIDTH + 2 * SGU_WIDTH)
WIN_GA = slice(SSM_WIDTH + 2 * SGU_WIDTH, SSM_WIDTH + 2 * SGU_WIDTH + D_MODEL)
WIN_GB = slice(SSM_WIDTH + 2 * SGU_WIDTH + D_MODEL, SSM_WIDTH + 2 * SGU_WIDTH + 2 * D_MODEL)
FF_TILE = 768
FF_BOUNDS = list(range(0, D_FF, FF_TILE)) + [D_FF]
N_FF_TILES = len(FF_BOUNDS) - 1
VMEM_LIMIT = 60 * 1024 * 1024

F32 = jnp.float32
BF16 = jnp.bfloat16


def _dot(a, b):
    return jnp.dot(a, b, preferred_element_type=F32)


def _rms(x, g):
    return x * lax.rsqrt(jnp.mean(x * x, axis=-1, keepdims=True) + EPS) * g


def _lane_tiles(ref, rows, n):
    return jnp.concatenate([ref[k, rows, :] for k in range(n)], axis=1)


def _prep_kernel(are_ref, aim_ref, ldt_ref, bre_ref, bim_ref, ws_ref,
                 abre_ref, abim_ref, bbre_ref, bbim_ref, wsm_ref):
    dt = jnp.exp(ldt_ref[...])
    ar = are_ref[...]
    ai = aim_ref[...]
    mag = jnp.exp(dt * ar)
    abar_re = mag * jnp.cos(dt * ai)
    abar_im = mag * jnp.sin(dt * ai)
    den = ar * ar + ai * ai
    nr = abar_re - 1.0
    ni = abar_im
    f_re = (nr * ar + ni * ai) / den
    f_im = (ni * ar - nr * ai) / den
    abre_ref[...] = abar_re
    abim_ref[...] = abar_im
    br = bre_ref[...]
    bi = bim_ref[...]
    fr = f_re[:, None, :]
    fi = f_im[:, None, :]
    bbre_ref[...] = fr * br - fi * bi
    bbim_ref[...] = fr * bi + fi * br
    row = lax.broadcasted_iota(jnp.int32, (SGU_GROUPS, CHUNK, CHUNK), 1)
    col = lax.broadcasted_iota(jnp.int32, (SGU_GROUPS, CHUNK, CHUNK), 2)
    wsm_ref[...] = jnp.where(row >= col, ws_ref[...], 0.0)


def _prep(a_re, a_im, log_dt, b_re_t, b_im_t, w_s):
    g, p, h = SSM_GROUPS, SSM_STATE, SSM_GROUP
    return pl.pallas_call(
        _prep_kernel,
        out_shape=(
            jax.ShapeDtypeStruct((g, p), F32),
            jax.ShapeDtypeStruct((g, p), F32),
            jax.ShapeDtypeStruct((g, h, p), F32),
            jax.ShapeDtypeStruct((g, h, p), F32),
            jax.ShapeDtypeStruct((SGU_GROUPS, CHUNK, CHUNK), F32),
        ),
        name="prep",
    )(a_re, a_im, log_dt, b_re_t, b_im_t, w_s)


def _mixer_kernel(x_ref, gmix_ref, win_ref,
                  bblk_ref, are_ref, aim_ref, cblk_ref, dskip_ref, wglu_ref, bglu_ref,
                  wpa_ref, gsgu_ref, wsp_ref, bsb_ref, wpb_ref, wout_ref,
                  o_ref,
                  h_bf, st_re, st_im, bu_re, bu_im, u_scr, ya_scr):
    @pl.when(pl.program_id(0) == 0)
    def _():
        st_re[...] = jnp.zeros_like(st_re)
        st_im[...] = jnp.zeros_like(st_im)

    def x_slab(j):
        return x_ref[j * SLAB_BATCHES:(j + 1) * SLAB_BATCHES].reshape(SLAB, D_MODEL)

    def slab_rows(j):
        return slice(j * SLAB, (j + 1) * SLAB)

    n_slabs = ROWS // SLAB

    for j in range(n_slabs):
        h = _rms(x_slab(j), gmix_ref[...]).astype(BF16)
        h_bf[slab_rows(j), :] = h
        u = _dot(h, win_ref[:, WIN_U])
        for e in range(SLAB_BATCHES):
            for k in range(SSM_TILES):
                u_scr[k, pl.ds(j * SLAB_BATCHES + e, CHUNK, stride=BATCH), :] = (
                    u[e * CHUNK:(e + 1) * CHUNK, k * LANES:(k + 1) * LANES])

    def sub_rows(j):
        return slice(j * SCAN_ROWS, (j + 1) * SCAN_ROWS)

    def load_u(j):
        return _lane_tiles(u_scr, sub_rows(j), SSM_TILES)

    def project_in(j):
        slot = j % 2
        ub = load_u(j).astype(BF16)
        for k in range(N_BLK):
            bu = _dot(ub[:, k * U_BLK:(k + 1) * U_BLK], bblk_ref[k])
            bu_re[slot, :, k * ST_BLK:(k + 1) * ST_BLK] = bu[:, :ST_BLK]
            bu_im[slot, :, k * ST_BLK:(k + 1) * ST_BLK] = bu[:, ST_BLK:]

    def recur(j, state):
        slot = j % 2
        state = list(state)
        for t in range(SCAN_STEPS):
            rr = slice(t * BATCH, (t + 1) * BATCH)
            for cb in range(STATE_COLS // SCAN_COLS):
                cs = slice(cb * SCAN_COLS, (cb + 1) * SCAN_COLS)
                sre, sim = state[cb]
                ar = are_ref[:, cs]
                ai = aim_ref[:, cs]
                nre = ar * sre - ai * sim + bu_re[slot, rr, cs]
                nim = ar * sim + ai * sre + bu_im[slot, rr, cs]
                bu_re[slot, rr, cs] = nre
                bu_im[slot, rr, cs] = nim
                state[cb] = (nre, nim)
        return state

    def project_out(j):
        slot = j % 2
        ys = []
        for k in range(N_BLK):
            sr = bu_re[slot, :, k * ST_BLK:(k + 1) * ST_BLK].astype(BF16)
            si = bu_im[slot, :, k * ST_BLK:(k + 1) * ST_BLK].astype(BF16)
            ys.append(_dot(sr, cblk_ref[k, :ST_BLK, :]) + _dot(si, cblk_ref[k, ST_BLK:, :]))
        y = jnp.concatenate(ys, axis=1) + dskip_ref[...] * load_u(j)
        y = jax.nn.gelu(y)
        y = y * jax.nn.sigmoid(_dot(y.astype(BF16), wglu_ref[...]) + bglu_ref[...])
        for k in range(SSM_TILES):
            ya_scr[k, sub_rows(j), :] = y[:, k * LANES:(k + 1) * LANES]

    n_sub = CHUNK // SCAN_STEPS
    state = [(st_re[:, cb * SCAN_COLS:(cb + 1) * SCAN_COLS],
              st_im[:, cb * SCAN_COLS:(cb + 1) * SCAN_COLS])
             for cb in range(STATE_COLS // SCAN_COLS)]
    def gate_dots(j):
        hs = h_bf[slab_rows(j), :]
        return (_dot(hs, win_ref[:, WIN_UV]), _dot(hs, win_ref[:, WIN_GB]),
                _dot(hs, win_ref[:, WIN_GA]))

    project_in(0)
    for j in range(n_sub):
        if j > 0:
            project_out(j - 1)
        if j + 1 < n_sub:
            project_in(j + 1)
        state = recur(j, state)
    project_out(n_sub - 1)
    for cb, (sre, sim) in enumerate(state):
        st_re[:, cb * SCAN_COLS:(cb + 1) * SCAN_COLS] = sre
        st_im[:, cb * SCAN_COLS:(cb + 1) * SCAN_COLS] = sim

    lane = lax.broadcasted_iota(jnp.int32, (CHUNK, LANES), 1)
    first_group = lane < SGU_GROUP_DIM

    def merge(j, uv, gb, ga):
        uv = jax.nn.gelu(uv)
        v = _rms(uv[:, SGU_WIDTH:], gsgu_ref[...])
        mixed = []
        for e in range(SLAB_BATCHES):
            tiles = []
            for k in range(SGU_TILES):
                vt = v[e * CHUNK:(e + 1) * CHUNK, k * LANES:(k + 1) * LANES]
                rhs = jnp.concatenate([jnp.where(first_group, vt, 0.0).astype(BF16),
                                       jnp.where(first_group, 0.0, vt).astype(BF16)], axis=0)
                tiles.append(_dot(wsp_ref[k], rhs))
            mixed.append(jnp.concatenate(tiles, axis=1) + bsb_ref[...])
        mixed = jnp.concatenate(mixed, axis=0)
        yb = _dot((uv[:, :SGU_WIDTH] * mixed).astype(BF16), wpb_ref[...])
        ya_in = jnp.concatenate([
            _lane_tiles(ya_scr, pl.ds(j * SLAB_BATCHES + e, CHUNK, stride=BATCH), SSM_TILES)
            for e in range(SLAB_BATCHES)], axis=0)
        ya = _dot(ya_in.astype(BF16), wpa_ref[...])
        m = jax.nn.sigmoid(ga) * ya + jax.nn.sigmoid(gb) * yb
        x1 = x_slab(j) + _dot(m.astype(BF16), wout_ref[...])
        for e in range(SLAB_BATCHES):
            for k in range(D_TILES):
                o_ref[k, pl.ds(j * SLAB_BATCHES + e, CHUNK, stride=BATCH), :] = (
                    x1[e * CHUNK:(e + 1) * CHUNK, k * LANES:(k + 1) * LANES])

    nxt = gate_dots(0)
    for j in range(n_slabs):
        cur = nxt
        if j + 1 < n_slabs:
            nxt = gate_dots(j + 1)
        merge(j, *cur)


def _const_spec(shape):
    nd = len(shape)
    return pl.BlockSpec(shape, lambda i, nd=nd: (0,) * nd, pipeline_mode=pl.Buffered(1))


def _mixer(x, consts):
    return pl.pallas_call(
        _mixer_kernel,
        grid=(N_STEPS,),
        in_specs=[pl.BlockSpec((BATCH, CHUNK, D_MODEL), lambda i: (0, i, 0))]
        + [_const_spec(c.shape) for c in consts],
        out_specs=pl.BlockSpec((D_TILES, ROWS, LANES), lambda i: (0, i, 0)),
        out_shape=jax.ShapeDtypeStruct((D_TILES, SEQ * BATCH, LANES), F32),
        scratch_shapes=[
            pltpu.VMEM((ROWS, D_MODEL), BF16),
            pltpu.VMEM((BATCH, STATE_COLS), F32),
            pltpu.VMEM((BATCH, STATE_COLS), F32),
            pltpu.VMEM((2, SCAN_ROWS, STATE_COLS), F32),
            pltpu.VMEM((2, SCAN_ROWS, STATE_COLS), F32),
            pltpu.VMEM((SSM_TILES, ROWS, LANES), F32),
            pltpu.VMEM((SSM_TILES, ROWS, LANES), F32),
        ],
        compiler_params=pltpu.CompilerParams(
            dimension_semantics=("arbitrary",), vmem_limit_bytes=VMEM_LIMIT),
        name="mixer",
    )(x, *consts)


def _ffn_kernel(x_ref, gffn_ref, wup_ref, cw_ref, cb_ref, wd_ref, gfin_ref, o_ref,
                h_bf, acc, carry, fin_scr):
    @pl.when(pl.program_id(0) == 0)
    def _():
        carry[...] = jnp.zeros_like(carry)

    def slab_rows(j):
        return slice(j * FFN_SLAB, (j + 1) * FFN_SLAB)

    n_slabs = ROWS // FFN_SLAB
    for j in range(n_slabs):
        h_bf[slab_rows(j), :] = _rms(
            _lane_tiles(x_ref, slab_rows(j), D_TILES), gffn_ref[...]).astype(BF16)

    def conv_tile(y, cols):
        prev = carry[:, cols]
        carry[:, cols] = y[ROWS - 2 * BATCH:, :]
        y1 = jnp.concatenate([prev[BATCH:, :], y[:ROWS - BATCH, :]], axis=0)
        y2 = jnp.concatenate([prev, y[:ROWS - 2 * BATCH, :]], axis=0)
        return (y2 * cw_ref[0:1, cols] + y1 * cw_ref[1:2, cols] + y * cw_ref[2:3, cols]
                + cb_ref[:, cols])

    hb = h_bf[...]

    def cols_a(c):
        return slice(FF_BOUNDS[c], FF_BOUNDS[c + 1])

    def cols_b(c):
        return slice(D_FF + FF_BOUNDS[c], D_FF + FF_BOUNDS[c + 1])

    def up(c):
        return _dot(hb, wup_ref[:, cols_a(c)]), _dot(hb, wup_ref[:, cols_b(c)])

    nxt = up(0)
    for c in range(N_FF_TILES):
        ya, yb = nxt
        if c + 1 < N_FF_TILES:
            nxt = up(c + 1)
        a = conv_tile(ya, cols_a(c))
        b = conv_tile(yb, cols_b(c))
        gated = (jax.nn.silu(a) * b).astype(BF16)
        down = _dot(gated, wd_ref[cols_a(c), :])
        if c == 0:
            acc[...] = down
        else:
            acc[...] += down

    for j in range(n_slabs):
        rows = slab_rows(j)
        y = _rms(_lane_tiles(x_ref, rows, D_TILES) + acc[rows, :], gfin_ref[...])
        for k in range(D_TILES):
            fin_scr[k, rows, :] = y[:, k * LANES:(k + 1) * LANES]

    for b in range(BATCH):
        for k in range(D_TILES):
            o_ref[b, :, k * LANES:(k + 1) * LANES] = fin_scr[k, pl.ds(b, CHUNK, stride=BATCH), :]


def _ffn(x1, consts):
    return pl.pallas_call(
        _ffn_kernel,
        grid=(N_STEPS,),
        in_specs=[pl.BlockSpec((D_TILES, ROWS, LANES), lambda i: (0, i, 0))]
        + [_const_spec(c.shape) for c in consts],
        out_specs=pl.BlockSpec((BATCH, CHUNK, D_MODEL), lambda i: (0, i, 0)),
        out_shape=jax.ShapeDtypeStruct((BATCH, SEQ, D_MODEL), F32),
        scratch_shapes=[
            pltpu.VMEM((ROWS, D_MODEL), BF16),
            pltpu.VMEM((ROWS, D_MODEL), F32),
            pltpu.VMEM((2 * BATCH, 2 * D_FF), F32),
            pltpu.VMEM((D_TILES, ROWS, LANES), F32),
        ],
        compiler_params=pltpu.CompilerParams(
            dimension_semantics=("arbitrary",), vmem_limit_bytes=VMEM_LIMIT),
        name="ffn",
    )(x1, *consts)


def kernel(x, g_mix, w_in, a_re, a_im, log_dt, b_re, b_im, c_re, c_im, d_skip, w_glu, b_glu, w_proj_a, g_sgu, w_s, b_s, w_proj_b, w_out, g_ffn, w_up, conv_w, conv_b, w_down, g_final):
    l = 0
    abar_re, abar_im, bbar_re, bbar_im, ws_m = _prep(
        a_re[l], a_im[l], log_dt[l][:, None],
        b_re[l].transpose(0, 2, 1), b_im[l].transpose(0, 2, 1), w_s[l])
    are = jnp.broadcast_to(abar_re.reshape(1, STATE_COLS), (BATCH, STATE_COLS))
    aim = jnp.broadcast_to(abar_im.reshape(1, STATE_COLS), (BATCH, STATE_COLS))
    split = lambda re, im: jnp.stack([re, im]).reshape(2, N_BLK, GPB, SSM_GROUP, SSM_STATE)
    eye = jnp.eye(GPB, dtype=F32)
    bblk = jnp.einsum("rkghp,gq->kghrqp", split(bbar_re, bbar_im), eye).reshape(
        N_BLK, U_BLK, 2 * ST_BLK).astype(BF16)
    cblk = jnp.einsum("rkghp,gq->krgpqh", split(c_re[l], -c_im[l]), eye).reshape(
        N_BLK, 2 * ST_BLK, U_BLK).astype(BF16)
    wsp = ws_m.reshape(SGU_TILES, 2, CHUNK, CHUNK).transpose(0, 2, 1, 3).reshape(
        SGU_TILES, CHUNK, 2 * CHUNK).astype(BF16)
    bsb = jnp.repeat(b_s[l].T, SGU_GROUP_DIM, axis=1)
    mixer_consts = [
        g_mix[l][None, :], w_in[l].astype(BF16),
        bblk, are, aim, cblk, d_skip[l][None, :], w_glu[l].astype(BF16), b_glu[l][None, :],
        w_proj_a[l].astype(BF16), g_sgu[l][None, :], wsp, bsb,
        w_proj_b[l].astype(BF16), w_out[l].astype(BF16),
    ]
    ffn_consts = [
        g_ffn[l][None, :], w_up[l].astype(BF16), conv_w[l], conv_b[l][None, :],
        w_down[l].astype(BF16), g_final[None, :],
    ]
    x1 = _mixer(x, mixer_consts)
    return _ffn(x1, ffn_consts)
```

```python
import jax
import jax.numpy as jnp
from jax import lax
from jax.experimental import pallas as pl
from jax.experimental.pallas import tpu as pltpu

D_MODEL = 1024
BATCH = 8
SEQ = 4096
SSM_WIDTH = 512
SSM_GROUP = 16
SSM_GROUPS = 32
SSM_STATE = 64
STATE_COLS = SSM_GROUPS * SSM_STATE
SGU_WIDTH = 512
SGU_GROUPS = 8
SGU_GROUP_DIM = 64
CHUNK = 128
LANES = 128
D_TILES = D_MODEL // LANES
SSM_TILES = SSM_WIDTH // LANES
SGU_TILES = SGU_WIDTH // LANES
D_FF = 2816
CONV_WIDTH = 3
EPS = 1e-6

ROWS = CHUNK * BATCH
N_STEPS = SEQ // CHUNK
SLAB = 512
SLAB_BATCHES = SLAB // CHUNK
FFN_SLAB = 256
SCAN_STEPS = 32
SCAN_ROWS = SCAN_STEPS * BATCH
SCAN_COLS = 1024
U_BLK = 128
GPB = U_BLK // SSM_GROUP
ST_BLK = GPB * SSM_STATE
N_BLK = SSM_WIDTH // U_BLK
WIN_U = slice(0, SSM_WIDTH)
WIN_UV = slice(SSM_WIDTH, SSM_WIDTH + 2 * SGU_WIDTH)
WIN_GA = slice(SSM_WIDTH + 2 * SGU_WIDTH, SSM_WIDTH + 2 * SGU_WIDTH + D_MODEL)
WIN_GB = slice(SSM_WIDTH + 2 * SGU_WIDTH + D_MODEL, SSM_WIDTH + 2 * SGU_WIDTH + 2 * D_MODEL)
FF_TILE = 768
FF_BOUNDS = list(range(0, D_FF, FF_TILE)) + [D_FF]
N_FF_TILES = len(FF_BOUNDS) - 1
VMEM_LIMIT = 60 * 1024 * 1024

F32 = jnp.float32
BF16 = jnp.bfloat16


def _dot(a, b):
    return jnp.dot(a, b, preferred_element_type=F32)


def _rms(x, g):
    return x * lax.rsqrt(jnp.mean(x * x, axis=-1, keepdims=True) + EPS) * g


def _lane_tiles(ref, rows, n):
    return jnp.concatenate([ref[k, rows, :] for k in range(n)], axis=1)


def _prep_kernel(are_ref, aim_ref, ldt_ref, bre_ref, bim_ref, ws_ref,
                 abre_ref, abim_ref, bbre_ref, bbim_ref, wsm_ref):
    dt = jnp.exp(ldt_ref[...])
    ar = are_ref[...]
    ai = aim_ref[...]
    mag = jnp.exp(dt * ar)
    abar_re = mag * jnp.cos(dt * ai)
    abar_im = mag * jnp.sin(dt * ai)
    den = ar * ar + ai * ai
    nr = abar_re - 1.0
    ni = abar_im
    f_re = (nr * ar + ni * ai) / den
    f_im = (ni * ar - nr * ai) / den
    abre_ref[...] = abar_re
    abim_ref[...] = abar_im
    br = bre_ref[...]
    bi = bim_ref[...]
    fr = f_re[:, None, :]
    fi = f_im[:, None, :]
    bbre_ref[...] = fr * br - fi * bi
    bbim_ref[...] = fr * bi + fi * br
    row = lax.broadcasted_iota(jnp.int32, (SGU_GROUPS, CHUNK, CHUNK), 1)
    col = lax.broadcasted_iota(jnp.int32, (SGU_GROUPS, CHUNK, CHUNK), 2)
    wsm_ref[...] = jnp.where(row >= col, ws_ref[...], 0.0)


def _prep(a_re, a_im, log_dt, b_re_t, b_im_t, w_s):
    g, p, h = SSM_GROUPS, SSM_STATE, SSM_GROUP
    return pl.pallas_call(
        _prep_kernel,
        out_shape=(
            jax.ShapeDtypeStruct((g, p), F32),
            jax.ShapeDtypeStruct((g, p), F32),
            jax.ShapeDtypeStruct((g, h, p), F32),
            jax.ShapeDtypeStruct((g, h, p), F32),
            jax.ShapeDtypeStruct((SGU_GROUPS, CHUNK, CHUNK), F32),
        ),
        name="prep",
    )(a_re, a_im, log_dt, b_re_t, b_im_t, w_s)


def _mixer_kernel(x_ref, gmix_ref, win_ref,
                  bblk_ref, are_ref, aim_ref, cblk_ref, dskip_ref, wglu_ref, bglu_ref,
                  wpa_ref, gsgu_ref, wsp_ref, bsb_ref, wpb_ref, wout_ref,
                  o_ref,
                  h_bf, st_re, st_im, bu_re, bu_im, u_scr, ya_scr):
    @pl.when(pl.program_id(0) == 0)
    def _():
        st_re[...] = jnp.zeros_like(st_re)
        st_im[...] = jnp.zeros_like(st_im)

    def x_slab(j):
        return x_ref[j * SLAB_BATCHES:(j + 1) * SLAB_BATCHES].reshape(SLAB, D_MODEL)

    def slab_rows(j):
        return slice(j * SLAB, (j + 1) * SLAB)

    n_slabs = ROWS // SLAB

    for j in range(n_slabs):
        h = _rms(x_slab(j), gmix_ref[...]).astype(BF16)
        h_bf[slab_rows(j), :] = h
        u = _dot(h, win_ref[:, WIN_U])
        for e in range(SLAB_BATCHES):
            for k in range(SSM_TILES):
                u_scr[k, pl.ds(j * SLAB_BATCHES + e, CHUNK, stride=BATCH), :] = (
                    u[e * CHUNK:(e + 1) * CHUNK, k * LANES:(k + 1) * LANES])

    def sub_rows(j):
        return slice(j * SCAN_ROWS, (j + 1) * SCAN_ROWS)

    def load_u(j):
        return _lane_tiles(u_scr, sub_rows(j), SSM_TILES)

    def project_in(j):
        slot = j % 2
        ub = load_u(j).astype(BF16)
        for k in range(N_BLK):
            bu = _dot(ub[:, k * U_BLK:(k + 1) * U_BLK], bblk_ref[k])
            bu_re[slot, :, k * ST_BLK:(k + 1) * ST_BLK] = bu[:, :ST_BLK]
            bu_im[slot, :, k * ST_BLK:(k + 1) * ST_BLK] = bu[:, ST_BLK:]

    def recur(j, state):
        slot = j % 2
        state = list(state)
        for t in range(SCAN_STEPS):
            rr = slice(t * BATCH, (t + 1) * BATCH)
            for cb in range(STATE_COLS // SCAN_COLS):
                cs = slice(cb * SCAN_COLS, (cb + 1) * SCAN_COLS)
                sre, sim = state[cb]
                ar = are_ref[:, cs]
                ai = aim_ref[:, cs]
                nre = ar * sre - ai * sim + bu_re[slot, rr, cs]
                nim = ar * sim + ai * sre + bu_im[slot, rr, cs]
                bu_re[slot, rr, cs] = nre
                bu_im[slot, rr, cs] = nim
                state[cb] = (nre, nim)
        return state

    def project_out(j):
        slot = j % 2
        ys = []
        for k in range(N_BLK):
            sr = bu_re[slot, :, k * ST_BLK:(k + 1) * ST_BLK].astype(BF16)
            si = bu_im[slot, :, k * ST_BLK:(k + 1) * ST_BLK].astype(BF16)
            ys.append(_dot(sr, cblk_ref[k, :ST_BLK, :]) + _dot(si, cblk_ref[k, ST_BLK:, :]))
        y = jnp.concatenate(ys, axis=1) + dskip_ref[...] * load_u(j)
        y = jax.nn.gelu(y)
        y = y * jax.nn.sigmoid(_dot(y.astype(BF16), wglu_ref[...]) + bglu_ref[...])
        for k in range(SSM_TILES):
            ya_scr[k, sub_rows(j), :] = y[:, k * LANES:(k + 1) * LANES]

    n_sub = CHUNK // SCAN_STEPS
    state = [(st_re[:, cb * SCAN_COLS:(cb + 1) * SCAN_COLS],
              st_im[:, cb * SCAN_COLS:(cb + 1) * SCAN_COLS])
             for cb in range(STATE_COLS // SCAN_COLS)]
    def gate_dots(j):
        hs = h_bf[slab_rows(j), :]
        return (_dot(hs, win_ref[:, WIN_UV]), _dot(hs, win_ref[:, WIN_GB]),
                _dot(hs, win_ref[:, WIN_GA]))

    project_in(0)
    for j in range(n_sub):
        if j > 0:
            project_out(j - 1)
        if j + 1 < n_sub:
            project_in(j + 1)
        state = recur(j, state)
    project_out(n_sub - 1)
    for cb, (sre, sim) in enumerate(state):
        st_re[:, cb * SCAN_COLS:(cb + 1) * SCAN_COLS] = sre
        st_im[:, cb * SCAN_COLS:(cb + 1) * SCAN_COLS] = sim

    lane = lax.broadcasted_iota(jnp.int32, (CHUNK, LANES), 1)
    first_group = lane < SGU_GROUP_DIM

    def merge(j, uv, gb, ga):
        uv = jax.nn.gelu(uv)
        v = _rms(uv[:, SGU_WIDTH:], gsgu_ref[...])
        tiles = []
        for k in range(SGU_TILES):
            rhs = []
            for e in range(SLAB_BATCHES):
                vt = v[e * CHUNK:(e + 1) * CHUNK, k * LANES:(k + 1) * LANES]
                rhs.append(jnp.concatenate([jnp.where(first_group, vt, 0.0).astype(BF16),
                                            jnp.where(first_group, 0.0, vt).astype(BF16)], axis=0))
            tiles.append(_dot(wsp_ref[k], jnp.concatenate(rhs, axis=1)))
        mixed = jnp.concatenate([
            jnp.concatenate([t[:, e * LANES:(e + 1) * LANES] for t in tiles], axis=1) + bsb_ref[...]
            for e in range(SLAB_BATCHES)], axis=0)
        yb = _dot((uv[:, :SGU_WIDTH] * mixed).astype(BF16), wpb_ref[...])
        ya_in = jnp.concatenate([
            _lane_tiles(ya_scr, pl.ds(j * SLAB_BATCHES + e, CHUNK, stride=BATCH), SSM_TILES)
            for e in range(SLAB_BATCHES)], axis=0)
        ya = _dot(ya_in.astype(BF16), wpa_ref[...])
        m = jax.nn.sigmoid(ga) * ya + jax.nn.sigmoid(gb) * yb
        x1 = x_slab(j) + _dot(m.astype(BF16), wout_ref[...])
        for e in range(SLAB_BATCHES):
            for k in range(D_TILES):
                o_ref[k, pl.ds(j * SLAB_BATCHES + e, CHUNK, stride=BATCH), :] = (
                    x1[e * CHUNK:(e + 1) * CHUNK, k * LANES:(k + 1) * LANES])

    nxt = gate_dots(0)
    for j in range(n_slabs):
        cur = nxt
        if j + 1 < n_slabs:
            nxt = gate_dots(j + 1)
        merge(j, *cur)


def _const_spec(shape):
    nd = len(shape)
    return pl.BlockSpec(shape, lambda i, nd=nd: (0,) * nd, pipeline_mode=pl.Buffered(1))


def _mixer(x, consts):
    return pl.pallas_call(
        _mixer_kernel,
        grid=(N_STEPS,),
        in_specs=[pl.BlockSpec((BATCH, CHUNK, D_MODEL), lambda i: (0, i, 0))]
        + [_const_spec(c.shape) for c in consts],
        out_specs=pl.BlockSpec((D_TILES, ROWS, LANES), lambda i: (0, i, 0)),
        out_shape=jax.ShapeDtypeStruct((D_TILES, SEQ * BATCH, LANES), F32),
        scratch_shapes=[
            pltpu.VMEM((ROWS, D_MODEL), BF16),
            pltpu.VMEM((BATCH, STATE_COLS), F32),
            pltpu.VMEM((BATCH, STATE_COLS), F32),
            pltpu.VMEM((2, SCAN_ROWS, STATE_COLS), F32),
            pltpu.VMEM((2, SCAN_ROWS, STATE_COLS), F32),
            pltpu.VMEM((SSM_TILES, ROWS, LANES), F32),
            pltpu.VMEM((SSM_TILES, ROWS, LANES), F32),
        ],
        compiler_params=pltpu.CompilerParams(
            dimension_semantics=("arbitrary",), vmem_limit_bytes=VMEM_LIMIT),
        name="mixer",
    )(x, *consts)


def _ffn_kernel(x_ref, gffn_ref, wup_ref, cw_ref, cb_ref, wd_ref, gfin_ref, o_ref,
                h_bf, acc, carry, fin_scr):
    @pl.when(pl.program_id(0) == 0)
    def _():
        carry[...] = jnp.zeros_like(carry)

    def slab_rows(j):
        return slice(j * FFN_SLAB, (j + 1) * FFN_SLAB)

    n_slabs = ROWS // FFN_SLAB
    for j in range(n_slabs):
        h_bf[slab_rows(j), :] = _rms(
            _lane_tiles(x_ref, slab_rows(j), D_TILES), gffn_ref[...]).astype(BF16)

    def conv_tile(y, cols):
        prev = carry[:, cols]
        carry[:, cols] = y[ROWS - 2 * BATCH:, :]
        y1 = jnp.concatenate([prev[BATCH:, :], y[:ROWS - BATCH, :]], axis=0)
        y2 = jnp.concatenate([prev, y[:ROWS - 2 * BATCH, :]], axis=0)
        return (y2 * cw_ref[0:1, cols] + y1 * cw_ref[1:2, cols] + y * cw_ref[2:3, cols]
                + cb_ref[:, cols])

    hb = h_bf[...]

    def cols_a(c):
        return slice(FF_BOUNDS[c], FF_BOUNDS[c + 1])

    def cols_b(c):
        return slice(D_FF + FF_BOUNDS[c], D_FF + FF_BOUNDS[c + 1])

    def up(c):
        return _dot(hb, wup_ref[:, cols_a(c)]), _dot(hb, wup_ref[:, cols_b(c)])

    nxt = up(0)
    for c in range(N_FF_TILES):
        ya, yb = nxt
        if c + 1 < N_FF_TILES:
            nxt = up(c + 1)
        a = conv_tile(ya, cols_a(c))
        b = conv_tile(yb, cols_b(c))
        gated = (jax.nn.silu(a) * b).astype(BF16)
        down = _dot(gated, wd_ref[cols_a(c), :])
        if c == 0:
            acc[...] = down
        else:
            acc[...] += down

    for j in range(n_slabs):
        rows = slab_rows(j)
        y = _rms(_lane_tiles(x_ref, rows, D_TILES) + acc[rows, :], gfin_ref[...])
        for k in range(D_TILES):
            fin_scr[k, rows, :] = y[:, k * LANES:(k + 1) * LANES]

    for b in range(BATCH):
        for k in range(D_TILES):
            o_ref[b, :, k * LANES:(k + 1) * LANES] = fin_scr[k, pl.ds(b, CHUNK, stride=BATCH), :]


def _ffn(x1, consts):
    return pl.pallas_call(
        _ffn_kernel,
        grid=(N_STEPS,),
        in_specs=[pl.BlockSpec((D_TILES, ROWS, LANES), lambda i: (0, i, 0))]
        + [_const_spec(c.shape) for c in consts],
        out_specs=pl.BlockSpec((BATCH, CHUNK, D_MODEL), lambda i: (0, i, 0)),
        out_shape=jax.ShapeDtypeStruct((BATCH, SEQ, D_MODEL), F32),
        scratch_shapes=[
            pltpu.VMEM((ROWS, D_MODEL), BF16),
            pltpu.VMEM((ROWS, D_MODEL), F32),
            pltpu.VMEM((2 * BATCH, 2 * D_FF), F32),
            pltpu.VMEM((D_TILES, ROWS, LANES), F32),
        ],
        compiler_params=pltpu.CompilerParams(
            dimension_semantics=("arbitrary",), vmem_limit_bytes=VMEM_LIMIT),
        name="ffn",
    )(x1, *consts)


def kernel(x, g_mix, w_in, a_re, a_im, log_dt, b_re, b_im, c_re, c_im, d_skip, w_glu, b_glu, w_proj_a, g_sgu, w_s, b_s, w_proj_b, w_out, g_ffn, w_up, conv_w, conv_b, w_down, g_final):
    l = 0
    abar_re, abar_im, bbar_re, bbar_im, ws_m = _prep(
        a_re[l], a_im[l], log_dt[l][:, None],
        b_re[l].transpose(0, 2, 1), b_im[l].transpose(0, 2, 1), w_s[l])
    are = jnp.broadcast_to(abar_re.reshape(1, STATE_COLS), (BATCH, STATE_COLS))
    aim = jnp.broadcast_to(abar_im.reshape(1, STATE_COLS), (BATCH, STATE_COLS))
    split = lambda re, im: jnp.stack([re, im]).reshape(2, N_BLK, GPB, SSM_GROUP, SSM_STATE)
    eye = jnp.eye(GPB, dtype=F32)
    bblk = jnp.einsum("rkghp,gq->kghrqp", split(bbar_re, bbar_im), eye).reshape(
        N_BLK, U_BLK, 2 * ST_BLK).astype(BF16)
    cblk = jnp.einsum("rkghp,gq->krgpqh", split(c_re[l], -c_im[l]), eye).reshape(
        N_BLK, 2 * ST_BLK, U_BLK).astype(BF16)
    wsp = ws_m.reshape(SGU_TILES, 2, CHUNK, CHUNK).transpose(0, 2, 1, 3).reshape(
        SGU_TILES, CHUNK, 2 * CHUNK).astype(BF16)
    bsb = jnp.repeat(b_s[l].T, SGU_GROUP_DIM, axis=1)
    mixer_consts = [
        g_mix[l][None, :], w_in[l].astype(BF16),
        bblk, are, aim, cblk, d_skip[l][None, :], w_glu[l].astype(BF16), b_glu[l][None, :],
        w_proj_a[l].astype(BF16), g_sgu[l][None, :], wsp, bsb,
        w_proj_b[l].astype(BF16), w_out[l].astype(BF16),
    ]
    ffn_consts = [
        g_ffn[l][None, :], w_up[l].astype(BF16), conv_w[l], conv_b[l][None, :],
        w_down[l].astype(BF16), g_final[None, :],
    ]
    x1 = _mixer(x, mixer_consts)
    return _ffn(x1, ffn_consts)
```

```python
import jax
import jax.numpy as jnp
from jax import lax
from jax.experimental import pallas as pl
from jax.experimental.pallas import tpu as pltpu

D_MODEL = 1024
BATCH = 8
SEQ = 4096
SSM_WIDTH = 512
SSM_GROUP = 16
SSM_GROUPS = 32
SSM_STATE = 64
STATE_COLS = SSM_GROUPS * SSM_STATE
SGU_WIDTH = 512
SGU_GROUPS = 8
SGU_GROUP_DIM = 64
CHUNK = 128
D_FF = 2816
CONV_WIDTH = 3
EPS = 1e-6

LANES = 128
SUBLANES = 8
VMEM_BYTES = 64 * 1024 * 1024
VMEM_RESERVE = 4 * 1024 * 1024
VMEM_LIMIT = VMEM_BYTES - VMEM_RESERVE

assert BATCH == SUBLANES
assert CONV_WIDTH == 3

D_TILES = D_MODEL // LANES
SSM_TILES = SSM_WIDTH // LANES
SGU_TILES = SGU_WIDTH // LANES
ROWS = CHUNK * BATCH
N_STEPS = SEQ // CHUNK
SLAB = 512
SLAB_BATCHES = SLAB // CHUNK
FFN_SLAB = 256
SCAN_STEPS = 32
SCAN_ROWS = SCAN_STEPS * BATCH
SCAN_COLS = 1024
U_BLK = 128
GPB = U_BLK // SSM_GROUP
ST_BLK = GPB * SSM_STATE
N_BLK = SSM_WIDTH // U_BLK
WIN_U = slice(0, SSM_WIDTH)
WIN_UV = slice(SSM_WIDTH, SSM_WIDTH + 2 * SGU_WIDTH)
WIN_GA = slice(SSM_WIDTH + 2 * SGU_WIDTH, SSM_WIDTH + 2 * SGU_WIDTH + D_MODEL)
WIN_GB = slice(SSM_WIDTH + 2 * SGU_WIDTH + D_MODEL, SSM_WIDTH + 2 * SGU_WIDTH + 2 * D_MODEL)
FF_TILE = 768
FF_BOUNDS = list(range(0, D_FF, FF_TILE)) + [D_FF]
N_FF_TILES = len(FF_BOUNDS) - 1

F32 = jnp.float32
BF16 = jnp.bfloat16


def _dot(a, b):
    return jnp.dot(a, b, preferred_element_type=F32)


def _rms(x, g):
    return x * lax.rsqrt(jnp.mean(x * x, axis=-1, keepdims=True) + EPS) * g


def _lane_tiles(ref, rows, n):
    return jnp.concatenate([ref[k, rows, :] for k in range(n)], axis=1)


def _prep_kernel(are_ref, aim_ref, ldt_ref, bre_ref, bim_ref, ws_ref,
                 abre_ref, abim_ref, bbre_ref, bbim_ref, wsm_ref):
    dt = jnp.exp(ldt_ref[...])
    ar = are_ref[...]
    ai = aim_ref[...]
    mag = jnp.exp(dt * ar)
    abar_re = mag * jnp.cos(dt * ai)
    abar_im = mag * jnp.sin(dt * ai)
    den = ar * ar + ai * ai
    nr = abar_re - 1.0
    ni = abar_im
    f_re = (nr * ar + ni * ai) / den
    f_im = (ni * ar - nr * ai) / den
    abre_ref[...] = abar_re
    abim_ref[...] = abar_im
    br = bre_ref[...]
    bi = bim_ref[...]
    fr = f_re[:, None, :]
    fi = f_im[:, None, :]
    bbre_ref[...] = fr * br - fi * bi
    bbim_ref[...] = fr * bi + fi * br
    row = lax.broadcasted_iota(jnp.int32, (SGU_GROUPS, CHUNK, CHUNK), 1)
    col = lax.broadcasted_iota(jnp.int32, (SGU_GROUPS, CHUNK, CHUNK), 2)
    wsm_ref[...] = jnp.where(row >= col, ws_ref[...], 0.0)


def _prep(a_re, a_im, log_dt, b_re_t, b_im_t, w_s):
    g, p, h = SSM_GROUPS, SSM_STATE, SSM_GROUP
    return pl.pallas_call(
        _prep_kernel,
        out_shape=(
            jax.ShapeDtypeStruct((g, p), F32),
            jax.ShapeDtypeStruct((g, p), F32),
            jax.ShapeDtypeStruct((g, h, p), F32),
            jax.ShapeDtypeStruct((g, h, p), F32),
            jax.ShapeDtypeStruct((SGU_GROUPS, CHUNK, CHUNK), F32),
        ),
        name="prep",
    )(a_re, a_im, log_dt, b_re_t, b_im_t, w_s)


def _mixer_kernel(x_ref, gmix_ref, win_ref,
                  bblk_ref, are_ref, aim_ref, cblk_ref, dskip_ref, wglu_ref, bglu_ref,
                  wpa_ref, gsgu_ref, wsp_ref, bsb_ref, wpb_ref, wout_ref,
                  o_ref,
                  h_bf, st_re, st_im, bu_re, bu_im, u_scr, ya_scr):
    @pl.when(pl.program_id(0) == 0)
    def _():
        st_re[...] = jnp.zeros_like(st_re)
        st_im[...] = jnp.zeros_like(st_im)

    def x_slab(j):
        return x_ref[j * SLAB_BATCHES:(j + 1) * SLAB_BATCHES].reshape(SLAB, D_MODEL)

    def slab_rows(j):
        return slice(j * SLAB, (j + 1) * SLAB)

    n_slabs = ROWS // SLAB

    for j in range(n_slabs):
        h = _rms(x_slab(j), gmix_ref[...]).astype(BF16)
        h_bf[slab_rows(j), :] = h
        u = _dot(h, win_ref[:, WIN_U])
        for e in range(SLAB_BATCHES):
            for k in range(SSM_TILES):
                u_scr[k, pl.ds(j * SLAB_BATCHES + e, CHUNK, stride=BATCH), :] = (
                    u[e * CHUNK:(e + 1) * CHUNK, k * LANES:(k + 1) * LANES])

    def sub_rows(j):
        return slice(j * SCAN_ROWS, (j + 1) * SCAN_ROWS)

    def load_u(j):
        return _lane_tiles(u_scr, sub_rows(j), SSM_TILES)

    def project_in(j):
        slot = j % 2
        ub = load_u(j).astype(BF16)
        for k in range(N_BLK):
            bu = _dot(ub[:, k * U_BLK:(k + 1) * U_BLK], bblk_ref[k])
            bu_re[slot, :, k * ST_BLK:(k + 1) * ST_BLK] = bu[:, :ST_BLK]
            bu_im[slot, :, k * ST_BLK:(k + 1) * ST_BLK] = bu[:, ST_BLK:]

    def recur(j, state):
        slot = j % 2
        state = list(state)
        for t in range(SCAN_STEPS):
            rr = slice(t * BATCH, (t + 1) * BATCH)
            for cb in range(STATE_COLS // SCAN_COLS):
                cs = slice(cb * SCAN_COLS, (cb + 1) * SCAN_COLS)
                sre, sim = state[cb]
                ar = are_ref[:, cs]
                ai = aim_ref[:, cs]
                nre = ar * sre - ai * sim + bu_re[slot, rr, cs]
                nim = ar * sim + ai * sre + bu_im[slot, rr, cs]
                bu_re[slot, rr, cs] = nre
                bu_im[slot, rr, cs] = nim
                state[cb] = (nre, nim)
        return state

    def project_out(j):
        slot = j % 2
        ys = []
        for k in range(N_BLK):
            sr = bu_re[slot, :, k * ST_BLK:(k + 1) * ST_BLK].astype(BF16)
            si = bu_im[slot, :, k * ST_BLK:(k + 1) * ST_BLK].astype(BF16)
            ys.append(_dot(sr, cblk_ref[k, :ST_BLK, :]) + _dot(si, cblk_ref[k, ST_BLK:, :]))
        y = jnp.concatenate(ys, axis=1) + dskip_ref[...] * load_u(j)
        y = jax.nn.gelu(y)
        y = y * jax.nn.sigmoid(_dot(y.astype(BF16), wglu_ref[...]) + bglu_ref[...])
        for k in range(SSM_TILES):
            ya_scr[k, sub_rows(j), :] = y[:, k * LANES:(k + 1) * LANES]

    n_sub = CHUNK // SCAN_STEPS
    state = [(st_re[:, cb * SCAN_COLS:(cb + 1) * SCAN_COLS],
              st_im[:, cb * SCAN_COLS:(cb + 1) * SCAN_COLS])
             for cb in range(STATE_COLS // SCAN_COLS)]
    project_in(0)
    for j in range(n_sub):
        if j > 0:
            project_out(j - 1)
        if j + 1 < n_sub:
            project_in(j + 1)
        state = recur(j, state)
    project_out(n_sub - 1)
    for cb, (sre, sim) in enumerate(state):
        st_re[:, cb * SCAN_COLS:(cb + 1) * SCAN_COLS] = sre
        st_im[:, cb * SCAN_COLS:(cb + 1) * SCAN_COLS] = sim

    lane = lax.broadcasted_iota(jnp.int32, (CHUNK, LANES), 1)
    first_group = lane < SGU_GROUP_DIM

    def gate_dots(j):
        hs = h_bf[slab_rows(j), :]
        return (_dot(hs, win_ref[:, WIN_UV]), _dot(hs, win_ref[:, WIN_GB]),
                _dot(hs, win_ref[:, WIN_GA]))

    def merge(j, uv, gb, ga):
        uv = jax.nn.gelu(uv)
        v = _rms(uv[:, SGU_WIDTH:], gsgu_ref[...])
        tiles = []
        for k in range(SGU_TILES):
            rhs = []
            for e in range(SLAB_BATCHES):
                vt = v[e * CHUNK:(e + 1) * CHUNK, k * LANES:(k + 1) * LANES]
                rhs.append(jnp.concatenate([jnp.where(first_group, vt, 0.0).astype(BF16),
                                            jnp.where(first_group, 0.0, vt).astype(BF16)], axis=0))
            tiles.append(_dot(wsp_ref[k], jnp.concatenate(rhs, axis=1)))
        mixed = jnp.concatenate([
            jnp.concatenate([t[:, e * LANES:(e + 1) * LANES] for t in tiles], axis=1) + bsb_ref[...]
            for e in range(SLAB_BATCHES)], axis=0)
        yb = _dot((uv[:, :SGU_WIDTH] * mixed).astype(BF16), wpb_ref[...])
        ya_in = jnp.concatenate([
            _lane_tiles(ya_scr, pl.ds(j * SLAB_BATCHES + e, CHUNK, stride=BATCH), SSM_TILES)
            for e in range(SLAB_BATCHES)], axis=0)
        ya = _dot(ya_in.astype(BF16), wpa_ref[...])
        m = jax.nn.sigmoid(ga) * ya + jax.nn.sigmoid(gb) * yb
        x1 = x_slab(j) + _dot(m.astype(BF16), wout_ref[...])
        for e in range(SLAB_BATCHES):
            for k in range(D_TILES):
                o_ref[k, pl.ds(j * SLAB_BATCHES + e, CHUNK, stride=BATCH), :] = (
                    x1[e * CHUNK:(e + 1) * CHUNK, k * LANES:(k + 1) * LANES])

    nxt = gate_dots(0)
    for j in range(n_slabs):
        cur = nxt
        if j + 1 < n_slabs:
            nxt = gate_dots(j + 1)
        merge(j, *cur)


def _const_spec(shape):
    nd = len(shape)
    return pl.BlockSpec(shape, lambda i, nd=nd: (0,) * nd, pipeline_mode=pl.Buffered(1))


def _mixer(x, consts):
    return pl.pallas_call(
        _mixer_kernel,
        grid=(N_STEPS,),
        in_specs=[pl.BlockSpec((BATCH, CHUNK, D_MODEL), lambda i: (0, i, 0))]
        + [_const_spec(c.shape) for c in consts],
        out_specs=pl.BlockSpec((D_TILES, ROWS, LANES), lambda i: (0, i, 0)),
        out_shape=jax.ShapeDtypeStruct((D_TILES, SEQ * BATCH, LANES), F32),
        scratch_shapes=[
            pltpu.VMEM((ROWS, D_MODEL), BF16),
            pltpu.VMEM((BATCH, STATE_COLS), F32),
            pltpu.VMEM((BATCH, STATE_COLS), F32),
            pltpu.VMEM((2, SCAN_ROWS, STATE_COLS), F32),
            pltpu.VMEM((2, SCAN_ROWS, STATE_COLS), F32),
            pltpu.VMEM((SSM_TILES, ROWS, LANES), F32),
            pltpu.VMEM((SSM_TILES, ROWS, LANES), F32),
        ],
        compiler_params=pltpu.CompilerParams(
            dimension_semantics=("arbitrary",), vmem_limit_bytes=VMEM_LIMIT),
        name="mixer",
    )(x, *consts)


def _ffn_kernel(x_ref, gffn_ref, wup_ref, cw_ref, cb_ref, wd_ref, gfin_ref, o_ref,
                h_bf, acc, carry, fin_scr):
    @pl.when(pl.program_id(0) == 0)
    def _():
        carry[...] = jnp.zeros_like(carry)

    def slab_rows(j):
        return slice(j * FFN_SLAB, (j + 1) * FFN_SLAB)

    n_slabs = ROWS // FFN_SLAB
    for j in range(n_slabs):
        h_bf[slab_rows(j), :] = _rms(
            _lane_tiles(x_ref, slab_rows(j), D_TILES), gffn_ref[...]).astype(BF16)

    def conv_tile(y, cols):
        prev = carry[:, cols]
        carry[:, cols] = y[ROWS - 2 * BATCH:, :]
        y1 = jnp.concatenate([prev[BATCH:, :], y[:ROWS - BATCH, :]], axis=0)
        y2 = jnp.concatenate([prev, y[:ROWS - 2 * BATCH, :]], axis=0)
        return (y2 * cw_ref[0:1, cols] + y1 * cw_ref[1:2, cols] + y * cw_ref[2:3, cols]
                + cb_ref[:, cols])

    hb = h_bf[...]

    def cols_a(c):
        return slice(FF_BOUNDS[c], FF_BOUNDS[c + 1])

    def cols_b(c):
        return slice(D_FF + FF_BOUNDS[c], D_FF + FF_BOUNDS[c + 1])

    def up(c):
        return _dot(hb, wup_ref[:, cols_a(c)]), _dot(hb, wup_ref[:, cols_b(c)])

    nxt = up(0)
    for c in range(N_FF_TILES):
        ya, yb = nxt
        if c + 1 < N_FF_TILES:
            nxt = up(c + 1)
        a = conv_tile(ya, cols_a(c))
        b = conv_tile(yb, cols_b(c))
        gated = (jax.nn.silu(a) * b).astype(BF16)
        down = _dot(gated, wd_ref[cols_a(c), :])
        if c == 0:
            acc[...] = down
        else:
            acc[...] += down

    for j in range(n_slabs):
        rows = slab_rows(j)
        y = _rms(_lane_tiles(x_ref, rows, D_TILES) + acc[rows, :], gfin_ref[...])
        for k in range(D_TILES):
            fin_scr[k, rows, :] = y[:, k * LANES:(k + 1) * LANES]

    for b in range(BATCH):
        for k in range(D_TILES):
            o_ref[b, :, k * LANES:(k + 1) * LANES] = fin_scr[k, pl.ds(b, CHUNK, stride=BATCH), :]


def _ffn(x1, consts):
    return pl.pallas_call(
        _ffn_kernel,
        grid=(N_STEPS,),
        in_specs=[pl.BlockSpec((D_TILES, ROWS, LANES), lambda i: (0, i, 0))]
        + [_const_spec(c.shape) for c in consts],
        out_specs=pl.BlockSpec((BATCH, CHUNK, D_MODEL), lambda i: (0, i, 0)),
        out_shape=jax.ShapeDtypeStruct((BATCH, SEQ, D_MODEL), F32),
        scratch_shapes=[
            pltpu.VMEM((ROWS, D_MODEL), BF16),
            pltpu.VMEM((ROWS, D_MODEL), F32),
            pltpu.VMEM((2 * BATCH, 2 * D_FF), F32),
            pltpu.VMEM((D_TILES, ROWS, LANES), F32),
        ],
        compiler_params=pltpu.CompilerParams(
            dimension_semantics=("arbitrary",), vmem_limit_bytes=VMEM_LIMIT),
        name="ffn",
    )(x1, *consts)


def kernel(x, g_mix, w_in, a_re, a_im, log_dt, b_re, b_im, c_re, c_im, d_skip, w_glu, b_glu, w_proj_a, g_sgu, w_s, b_s, w_proj_b, w_out, g_ffn, w_up, conv_w, conv_b, w_down, g_final):
    l = 0
    abar_re, abar_im, bbar_re, bbar_im, ws_m = _prep(
        a_re[l], a_im[l], log_dt[l][:, None],
        b_re[l].transpose(0, 2, 1), b_im[l].transpose(0, 2, 1), w_s[l])
    are = jnp.broadcast_to(abar_re.reshape(1, STATE_COLS), (BATCH, STATE_COLS))
    aim = jnp.broadcast_to(abar_im.reshape(1, STATE_COLS), (BATCH, STATE_COLS))
    split = lambda re, im: jnp.stack([re, im]).reshape(2, N_BLK, GPB, SSM_GROUP, SSM_STATE)
    eye = jnp.eye(GPB, dtype=F32)
    bblk = jnp.einsum("rkghp,gq->kghrqp", split(bbar_re, bbar_im), eye).reshape(
        N_BLK, U_BLK, 2 * ST_BLK).astype(BF16)
    cblk = jnp.einsum("rkghp,gq->krgpqh", split(c_re[l], -c_im[l]), eye).reshape(
        N_BLK, 2 * ST_BLK, U_BLK).astype(BF16)
    wsp = ws_m.reshape(SGU_TILES, 2, CHUNK, CHUNK).transpose(0, 2, 1, 3).reshape(
        SGU_TILES, CHUNK, 2 * CHUNK).astype(BF16)
    bsb = jnp.repeat(b_s[l].T, SGU_GROUP_DIM, axis=1)
    mixer_consts = [
        g_mix[l][None, :], w_in[l].astype(BF16),
        bblk, are, aim, cblk, d_skip[l][None, :], w_glu[l].astype(BF16), b_glu[l][None, :],
        w_proj_a[l].astype(BF16), g_sgu[l][None, :], wsp, bsb,
        w_proj_b[l].astype(BF16), w_out[l].astype(BF16),
    ]
    ffn_consts = [
        g_ffn[l][None, :], w_up[l].astype(BF16), conv_w[l], conv_b[l][None, :],
        w_down[l].astype(BF16), g_final[None, :],
    ]
    x1 = _mixer(x, mixer_consts)
    return _ffn(x1, ffn_consts)
```

```python
import jax
import jax.numpy as jnp
from jax import lax
from jax.experimental import pallas as pl
from jax.experimental.pallas import tpu as pltpu

D_MODEL = 1024
BATCH = 8
SEQ = 4096
SSM_WIDTH = 512
SSM_GROUP = 16
SSM_GROUPS = 32
SSM_STATE = 64
STATE_COLS = SSM_GROUPS * SSM_STATE
SGU_WIDTH = 512
SGU_GROUPS = 8
SGU_GROUP_DIM = 64
CHUNK = 128
D_FF = 2816
CONV_WIDTH = 3
EPS = 1e-6

LANES = 128
SUBLANES = 8
VMEM_BYTES = 64 * 1024 * 1024
VMEM_RESERVE = 4 * 1024 * 1024
VMEM_LIMIT = VMEM_BYTES - VMEM_RESERVE

assert BATCH == SUBLANES
assert CONV_WIDTH == 3

D_TILES = D_MODEL // LANES
SSM_TILES = SSM_WIDTH // LANES
SGU_TILES = SGU_WIDTH // LANES
ROWS = CHUNK * BATCH
N_STEPS = SEQ // CHUNK
SLAB = 512
SLAB_BATCHES = SLAB // CHUNK
FFN_SLAB = 256
HALF = ROWS // 2
FFN_LAG = 2
SCAN_STEPS = 32
SCAN_ROWS = SCAN_STEPS * BATCH
SCAN_COLS = 1024
U_BLK = 128
GPB = U_BLK // SSM_GROUP
ST_BLK = GPB * SSM_STATE
N_BLK = SSM_WIDTH // U_BLK
WIN_U = slice(0, SSM_WIDTH)
WIN_UV = slice(SSM_WIDTH, SSM_WIDTH + 2 * SGU_WIDTH)
WIN_GA = slice(SSM_WIDTH + 2 * SGU_WIDTH, SSM_WIDTH + 2 * SGU_WIDTH + D_MODEL)
WIN_GB = slice(SSM_WIDTH + 2 * SGU_WIDTH + D_MODEL, SSM_WIDTH + 2 * SGU_WIDTH + 2 * D_MODEL)
FF_TILE = 768
FF_BOUNDS = list(range(0, D_FF, FF_TILE)) + [D_FF]
N_FF_TILES = len(FF_BOUNDS) - 1

F32 = jnp.float32
BF16 = jnp.bfloat16


def _dot(a, b):
    return jnp.dot(a, b, preferred_element_type=F32)


def _rms(x, g):
    return x * lax.rsqrt(jnp.mean(x * x, axis=-1, keepdims=True) + EPS) * g


def _lane_tiles(ref, rows, n):
    return jnp.concatenate([ref[k, rows, :] for k in range(n)], axis=1)


def _prep_kernel(are_ref, aim_ref, ldt_ref, bre_ref, bim_ref, ws_ref,
                 abre_ref, abim_ref, bbre_ref, bbim_ref, wsm_ref):
    dt = jnp.exp(ldt_ref[...])
    ar = are_ref[...]
    ai = aim_ref[...]
    mag = jnp.exp(dt * ar)
    abar_re = mag * jnp.cos(dt * ai)
    abar_im = mag * jnp.sin(dt * ai)
    den = ar * ar + ai * ai
    nr = abar_re - 1.0
    ni = abar_im
    f_re = (nr * ar + ni * ai) / den
    f_im = (ni * ar - nr * ai) / den
    abre_ref[...] = abar_re
    abim_ref[...] = abar_im
    br = bre_ref[...]
    bi = bim_ref[...]
    fr = f_re[:, None, :]
    fi = f_im[:, None, :]
    bbre_ref[...] = fr * br - fi * bi
    bbim_ref[...] = fr * bi + fi * br
    row = lax.broadcasted_iota(jnp.int32, (SGU_GROUPS, CHUNK, CHUNK), 1)
    col = lax.broadcasted_iota(jnp.int32, (SGU_GROUPS, CHUNK, CHUNK), 2)
    wsm_ref[...] = jnp.where(row >= col, ws_ref[...], 0.0)


def _prep(a_re, a_im, log_dt, b_re_t, b_im_t, w_s):
    g, p, h = SSM_GROUPS, SSM_STATE, SSM_GROUP
    return pl.pallas_call(
        _prep_kernel,
        out_shape=(
            jax.ShapeDtypeStruct((g, p), F32),
            jax.ShapeDtypeStruct((g, p), F32),
            jax.ShapeDtypeStruct((g, h, p), F32),
            jax.ShapeDtypeStruct((g, h, p), F32),
            jax.ShapeDtypeStruct((SGU_GROUPS, CHUNK, CHUNK), F32),
        ),
        name="prep",
    )(a_re, a_im, log_dt, b_re_t, b_im_t, w_s)


def _mixer_kernel(x_ref, gmix_ref, win_ref,
                  bblk_ref, are_ref, aim_ref, cblk_ref, dskip_ref, wglu_ref, bglu_ref,
                  wpa_ref, gsgu_ref, wsp_ref, bsb_ref, wpb_ref, wout_ref,
                  o_ref,
                  h_bf, st_re, st_im, bu_re, bu_im, u_scr, ya_scr):
    @pl.when(pl.program_id(0) == 0)
    def _():
        st_re[...] = jnp.zeros_like(st_re)
        st_im[...] = jnp.zeros_like(st_im)

    def x_slab(j):
        return x_ref[j * SLAB_BATCHES:(j + 1) * SLAB_BATCHES].reshape(SLAB, D_MODEL)

    def slab_rows(j):
        return slice(j * SLAB, (j + 1) * SLAB)

    n_slabs = ROWS // SLAB

    for j in range(n_slabs):
        h = _rms(x_slab(j), gmix_ref[...]).astype(BF16)
        h_bf[slab_rows(j), :] = h
        u = _dot(h, win_ref[:, WIN_U])
        for e in range(SLAB_BATCHES):
            for k in range(SSM_TILES):
                u_scr[k, pl.ds(j * SLAB_BATCHES + e, CHUNK, stride=BATCH), :] = (
                    u[e * CHUNK:(e + 1) * CHUNK, k * LANES:(k + 1) * LANES])

    def sub_rows(j):
        return slice(j * SCAN_ROWS, (j + 1) * SCAN_ROWS)

    def load_u(j):
        return _lane_tiles(u_scr, sub_rows(j), SSM_TILES)

    def project_in(j):
        slot = j % 2
        ub = load_u(j).astype(BF16)
        for k in range(N_BLK):
            bu = _dot(ub[:, k * U_BLK:(k + 1) * U_BLK], bblk_ref[k])
            bu_re[slot, :, k * ST_BLK:(k + 1) * ST_BLK] = bu[:, :ST_BLK]
            bu_im[slot, :, k * ST_BLK:(k + 1) * ST_BLK] = bu[:, ST_BLK:]

    def recur(j, state):
        slot = j % 2
        state = list(state)
        for t in range(SCAN_STEPS):
            rr = slice(t * BATCH, (t + 1) * BATCH)
            for cb in range(STATE_COLS // SCAN_COLS):
                cs = slice(cb * SCAN_COLS, (cb + 1) * SCAN_COLS)
                sre, sim = state[cb]
                ar = are_ref[:, cs]
                ai = aim_ref[:, cs]
                nre = ar * sre - ai * sim + bu_re[slot, rr, cs]
                nim = ar * sim + ai * sre + bu_im[slot, rr, cs]
                bu_re[slot, rr, cs] = nre
                bu_im[slot, rr, cs] = nim
                state[cb] = (nre, nim)
        return state

    def project_out(j):
        slot = j % 2
        ys = []
        for k in range(N_BLK):
            sr = bu_re[slot, :, k * ST_BLK:(k + 1) * ST_BLK].astype(BF16)
            si = bu_im[slot, :, k * ST_BLK:(k + 1) * ST_BLK].astype(BF16)
            ys.append(_dot(sr, cblk_ref[k, :ST_BLK, :]) + _dot(si, cblk_ref[k, ST_BLK:, :]))
        y = jnp.concatenate(ys, axis=1) + dskip_ref[...] * load_u(j)
        y = jax.nn.gelu(y)
        y = y * jax.nn.sigmoid(_dot(y.astype(BF16), wglu_ref[...]) + bglu_ref[...])
        for k in range(SSM_TILES):
            ya_scr[k, sub_rows(j), :] = y[:, k * LANES:(k + 1) * LANES]

    n_sub = CHUNK // SCAN_STEPS
    state = [(st_re[:, cb * SCAN_COLS:(cb + 1) * SCAN_COLS],
              st_im[:, cb * SCAN_COLS:(cb + 1) * SCAN_COLS])
             for cb in range(STATE_COLS // SCAN_COLS)]
    project_in(0)
    for j in range(n_sub):
        if j > 0:
            project_out(j - 1)
        if j + 1 < n_sub:
            project_in(j + 1)
        state = recur(j, state)
    project_out(n_sub - 1)
    for cb, (sre, sim) in enumerate(state):
        st_re[:, cb * SCAN_COLS:(cb + 1) * SCAN_COLS] = sre
        st_im[:, cb * SCAN_COLS:(cb + 1) * SCAN_COLS] = sim

    lane = lax.broadcasted_iota(jnp.int32, (CHUNK, LANES), 1)
    first_group = lane < SGU_GROUP_DIM

    def gate_dots(j):
        hs = h_bf[slab_rows(j), :]
        return (_dot(hs, win_ref[:, WIN_UV]), _dot(hs, win_ref[:, WIN_GB]),
                _dot(hs, win_ref[:, WIN_GA]))

    def merge(j, uv, gb, ga):
        uv = jax.nn.gelu(uv)
        v = _rms(uv[:, SGU_WIDTH:], gsgu_ref[...])
        tiles = []
        for k in range(SGU_TILES):
            rhs = []
            for e in range(SLAB_BATCHES):
                vt = v[e * CHUNK:(e + 1) * CHUNK, k * LANES:(k + 1) * LANES]
                rhs.append(jnp.concatenate([jnp.where(first_group, vt, 0.0).astype(BF16),
                                            jnp.where(first_group, 0.0, vt).astype(BF16)], axis=0))
            tiles.append(_dot(wsp_ref[k], jnp.concatenate(rhs, axis=1)))
        mixed = jnp.concatenate([
            jnp.concatenate([t[:, e * LANES:(e + 1) * LANES] for t in tiles], axis=1) + bsb_ref[...]
            for e in range(SLAB_BATCHES)], axis=0)
        yb = _dot((uv[:, :SGU_WIDTH] * mixed).astype(BF16), wpb_ref[...])
        ya_in = jnp.concatenate([
            _lane_tiles(ya_scr, pl.ds(j * SLAB_BATCHES + e, CHUNK, stride=BATCH), SSM_TILES)
            for e in range(SLAB_BATCHES)], axis=0)
        ya = _dot(ya_in.astype(BF16), wpa_ref[...])
        m = jax.nn.sigmoid(ga) * ya + jax.nn.sigmoid(gb) * yb
        x1 = x_slab(j) + _dot(m.astype(BF16), wout_ref[...])
        for e in range(SLAB_BATCHES):
            for k in range(D_TILES):
                o_ref[k, pl.ds(j * SLAB_BATCHES + e, CHUNK, stride=BATCH), :] = (
                    x1[e * CHUNK:(e + 1) * CHUNK, k * LANES:(k + 1) * LANES])

    nxt = gate_dots(0)
    for j in range(n_slabs):
        cur = nxt
        if j + 1 < n_slabs:
            nxt = gate_dots(j + 1)
        merge(j, *cur)


def _const_spec(shape):
    nd = len(shape)
    return pl.BlockSpec(shape, lambda i, nd=nd: (0,) * nd, pipeline_mode=pl.Buffered(1))


def _mixer(x, consts):
    return pl.pallas_call(
        _mixer_kernel,
        grid=(N_STEPS,),
        in_specs=[pl.BlockSpec((BATCH, CHUNK, D_MODEL), lambda i: (0, i, 0))]
        + [_const_spec(c.shape) for c in consts],
        out_specs=pl.BlockSpec((D_TILES, ROWS, LANES), lambda i: (0, i, 0)),
        out_shape=jax.ShapeDtypeStruct((D_TILES, SEQ * BATCH, LANES), F32),
        scratch_shapes=[
            pltpu.VMEM((ROWS, D_MODEL), BF16),
            pltpu.VMEM((BATCH, STATE_COLS), F32),
            pltpu.VMEM((BATCH, STATE_COLS), F32),
            pltpu.VMEM((2, SCAN_ROWS, STATE_COLS), F32),
            pltpu.VMEM((2, SCAN_ROWS, STATE_COLS), F32),
            pltpu.VMEM((SSM_TILES, ROWS, LANES), F32),
            pltpu.VMEM((SSM_TILES, ROWS, LANES), F32),
        ],
        compiler_params=pltpu.CompilerParams(
            dimension_semantics=("arbitrary",), vmem_limit_bytes=VMEM_LIMIT),
        name="mixer",
    )(x, *consts)


def _ffn_kernel(x_ref, gffn_ref, wup_ref, cw_ref, cb_ref, wd_ref, gfin_ref, o_ref,
                h_bf, acc, carry, fin_scr):
    @pl.when(pl.program_id(0) == 0)
    def _():
        carry[...] = jnp.zeros_like(carry)

    def slab_rows(j):
        return slice(j * FFN_SLAB, (j + 1) * FFN_SLAB)

    half_steps = HALF // BATCH
    last_tile = N_FF_TILES - 1

    def half_rows(r):
        return slice(r * HALF, (r + 1) * HALF)

    def cols_a(c):
        return slice(FF_BOUNDS[c], FF_BOUNDS[c + 1])

    def cols_b(c):
        return slice(D_FF + FF_BOUNDS[c], D_FF + FF_BOUNDS[c + 1])

    def norm_in(r):
        for j in range(HALF // FFN_SLAB):
            rows = slice(r * HALF + j * FFN_SLAB, r * HALF + (j + 1) * FFN_SLAB)
            h_bf[rows, :] = _rms(_lane_tiles(x_ref, rows, D_TILES), gffn_ref[...]).astype(BF16)

    def up(c, r):
        hb = h_bf[half_rows(r), :]
        return _dot(hb, wup_ref[:, cols_a(c)]), _dot(hb, wup_ref[:, cols_b(c)])

    def conv(y, prev, cols):
        y1 = jnp.concatenate([prev[BATCH:, :], y[:HALF - BATCH, :]], axis=0)
        y2 = jnp.concatenate([prev, y[:HALF - 2 * BATCH, :]], axis=0)
        return (y2 * cw_ref[0:1, cols] + y1 * cw_ref[1:2, cols] + y * cw_ref[2:3, cols]
                + cb_ref[:, cols])

    def norm_out(r, total):
        rows = half_rows(r)
        y = _rms(total, gfin_ref[...])
        for k in range(D_TILES):
            fin_scr[k, rows, :] = y[:, k * LANES:(k + 1) * LANES]
        for b in range(BATCH):
            for k in range(D_TILES):
                o_ref[b, r * half_steps:(r + 1) * half_steps, k * LANES:(k + 1) * LANES] = (
                    fin_scr[k, pl.ds(r * HALF + b, half_steps, stride=BATCH), :])

    order = []
    for s in range(N_FF_TILES + FFN_LAG):
        if s < N_FF_TILES:
            order.append((s, 0))
        if s >= FFN_LAG:
            order.append((s - FFN_LAG, 1))

    norm_in(0)
    nxt = up(0, 0)
    tails = {}
    for idx, (c, r) in enumerate(order):
        rows = half_rows(r)
        ya, yb = nxt
        if idx + 1 < len(order):
            c_next, r_next = order[idx + 1]
            if (c_next, r_next) == (0, 1):
                norm_in(1)
            nxt = up(c_next, r_next)
        if r == 0:
            prev_a, prev_b = carry[:, cols_a(c)], carry[:, cols_b(c)]
        else:
            prev_a, prev_b = tails[c]
            carry[:, cols_a(c)] = ya[HALF - 2 * BATCH:, :]
            carry[:, cols_b(c)] = yb[HALF - 2 * BATCH:, :]
        tails[c] = (ya[HALF - 2 * BATCH:, :], yb[HALF - 2 * BATCH:, :])
        a = conv(ya, prev_a, cols_a(c))
        b = conv(yb, prev_b, cols_b(c))
        down = _dot((jax.nn.silu(a) * b).astype(BF16), wd_ref[cols_a(c), :])
        if c == 0:
            acc[rows, :] = down
        elif c < last_tile:
            acc[rows, :] += down
        else:
            norm_out(r, _lane_tiles(x_ref, rows, D_TILES) + (acc[rows, :] + down))


def _ffn(x1, consts):
    return pl.pallas_call(
        _ffn_kernel,
        grid=(N_STEPS,),
        in_specs=[pl.BlockSpec((D_TILES, ROWS, LANES), lambda i: (0, i, 0))]
        + [_const_spec(c.shape) for c in consts],
        out_specs=pl.BlockSpec((BATCH, CHUNK, D_MODEL), lambda i: (0, i, 0)),
        out_shape=jax.ShapeDtypeStruct((BATCH, SEQ, D_MODEL), F32),
        scratch_shapes=[
            pltpu.VMEM((ROWS, D_MODEL), BF16),
            pltpu.VMEM((ROWS, D_MODEL), F32),
            pltpu.VMEM((2 * BATCH, 2 * D_FF), F32),
            pltpu.VMEM((D_TILES, ROWS, LANES), F32),
        ],
        compiler_params=pltpu.CompilerParams(
            dimension_semantics=("arbitrary",), vmem_limit_bytes=VMEM_LIMIT),
        name="ffn",
    )(x1, *consts)


def kernel(x, g_mix, w_in, a_re, a_im, log_dt, b_re, b_im, c_re, c_im, d_skip, w_glu, b_glu, w_proj_a, g_sgu, w_s, b_s, w_proj_b, w_out, g_ffn, w_up, conv_w, conv_b, w_down, g_final):
    l = 0
    abar_re, abar_im, bbar_re, bbar_im, ws_m = _prep(
        a_re[l], a_im[l], log_dt[l][:, None],
        b_re[l].transpose(0, 2, 1), b_im[l].transpose(0, 2, 1), w_s[l])
    are = jnp.broadcast_to(abar_re.reshape(1, STATE_COLS), (BATCH, STATE_COLS))
    aim = jnp.broadcast_to(abar_im.reshape(1, STATE_COLS), (BATCH, STATE_COLS))
    split = lambda re, im: jnp.stack([re, im]).reshape(2, N_BLK, GPB, SSM_GROUP, SSM_STATE)
    eye = jnp.eye(GPB, dtype=F32)
    bblk = jnp.einsum("rkghp,gq->kghrqp", split(bbar_re, bbar_im), eye).reshape(
        N_BLK, U_BLK, 2 * ST_BLK).astype(BF16)
    cblk = jnp.einsum("rkghp,gq->krgpqh", split(c_re[l], -c_im[l]), eye).reshape(
        N_BLK, 2 * ST_BLK, U_BLK).astype(BF16)
    wsp = ws_m.reshape(SGU_TILES, 2, CHUNK, CHUNK).transpose(0, 2, 1, 3).reshape(
        SGU_TILES, CHUNK, 2 * CHUNK).astype(BF16)
    bsb = jnp.repeat(b_s[l].T, SGU_GROUP_DIM, axis=1)
    mixer_consts = [
        g_mix[l][None, :], w_in[l].astype(BF16),
        bblk, are, aim, cblk, d_skip[l][None, :], w_glu[l].astype(BF16), b_glu[l][None, :],
        w_proj_a[l].astype(BF16), g_sgu[l][None, :], wsp, bsb,
        w_proj_b[l].astype(BF16), w_out[l].astype(BF16),
    ]
    ffn_consts = [
        g_ffn[l][None, :], w_up[l].astype(BF16), conv_w[l], conv_b[l][None, :],
        w_down[l].astype(BF16), g_final[None, :],
    ]
    x1 = _mixer(x, mixer_consts)
    return _ffn(x1, ffn_consts)
```

```python
import jax
import jax.numpy as jnp
from jax import lax
from jax.experimental import pallas as pl
from jax.experimental.pallas import tpu as pltpu

D_MODEL = 1024
BATCH = 8
SEQ = 4096
SSM_WIDTH = 512
SSM_GROUP = 16
SSM_GROUPS = 32
SSM_STATE = 64
STATE_COLS = SSM_GROUPS * SSM_STATE
SGU_WIDTH = 512
SGU_GROUPS = 8
SGU_GROUP_DIM = 64
CHUNK = 128
D_FF = 2816
CONV_WIDTH = 3
EPS = 1e-6

LANES = 128
SUBLANES = 8
VMEM_BYTES = 64 * 1024 * 1024
VMEM_RESERVE = 4 * 1024 * 1024
VMEM_LIMIT = VMEM_BYTES - VMEM_RESERVE

assert BATCH == SUBLANES
assert CONV_WIDTH == 3

D_TILES = D_MODEL // LANES
SSM_TILES = SSM_WIDTH // LANES
SGU_TILES = SGU_WIDTH // LANES
ROWS = CHUNK * BATCH
N_STEPS = SEQ // CHUNK
SLAB = 512
SLAB_BATCHES = SLAB // CHUNK
FFN_SLAB = 256
HALF = ROWS // 2
FFN_LAG = 1
SCAN_STEPS = 32
SCAN_ROWS = SCAN_STEPS * BATCH
SCAN_COLS = 1024
U_BLK = 128
GPB = U_BLK // SSM_GROUP
ST_BLK = GPB * SSM_STATE
N_BLK = SSM_WIDTH // U_BLK
WIN_U = slice(0, SSM_WIDTH)
WIN_UV = slice(SSM_WIDTH, SSM_WIDTH + 2 * SGU_WIDTH)
WIN_GA = slice(SSM_WIDTH + 2 * SGU_WIDTH, SSM_WIDTH + 2 * SGU_WIDTH + D_MODEL)
WIN_GB = slice(SSM_WIDTH + 2 * SGU_WIDTH + D_MODEL, SSM_WIDTH + 2 * SGU_WIDTH + 2 * D_MODEL)
FF_TILE = 768
FF_BOUNDS = list(range(0, D_FF, FF_TILE)) + [D_FF]
N_FF_TILES = len(FF_BOUNDS) - 1

F32 = jnp.float32
BF16 = jnp.bfloat16


def _dot(a, b):
    return jnp.dot(a, b, preferred_element_type=F32)


def _rms(x, g):
    return x * lax.rsqrt(jnp.mean(x * x, axis=-1, keepdims=True) + EPS) * g


def _lane_tiles(ref, rows, n):
    return jnp.concatenate([ref[k, rows, :] for k in range(n)], axis=1)


def _prep_kernel(are_ref, aim_ref, ldt_ref, bre_ref, bim_ref, ws_ref,
                 abre_ref, abim_ref, bbre_ref, bbim_ref, wsm_ref):
    dt = jnp.exp(ldt_ref[...])
    ar = are_ref[...]
    ai = aim_ref[...]
    mag = jnp.exp(dt * ar)
    abar_re = mag * jnp.cos(dt * ai)
    abar_im = mag * jnp.sin(dt * ai)
    den = ar * ar + ai * ai
    nr = abar_re - 1.0
    ni = abar_im
    f_re = (nr * ar + ni * ai) / den
    f_im = (ni * ar - nr * ai) / den
    abre_ref[...] = abar_re
    abim_ref[...] = abar_im
    br = bre_ref[...]
    bi = bim_ref[...]
    fr = f_re[:, None, :]
    fi = f_im[:, None, :]
    bbre_ref[...] = fr * br - fi * bi
    bbim_ref[...] = fr * bi + fi * br
    row = lax.broadcasted_iota(jnp.int32, (SGU_GROUPS, CHUNK, CHUNK), 1)
    col = lax.broadcasted_iota(jnp.int32, (SGU_GROUPS, CHUNK, CHUNK), 2)
    wsm_ref[...] = jnp.where(row >= col, ws_ref[...], 0.0)


def _prep(a_re, a_im, log_dt, b_re_t, b_im_t, w_s):
    g, p, h = SSM_GROUPS, SSM_STATE, SSM_GROUP
    return pl.pallas_call(
        _prep_kernel,
        out_shape=(
            jax.ShapeDtypeStruct((g, p), F32),
            jax.ShapeDtypeStruct((g, p), F32),
            jax.ShapeDtypeStruct((g, h, p), F32),
            jax.ShapeDtypeStruct((g, h, p), F32),
            jax.ShapeDtypeStruct((SGU_GROUPS, CHUNK, CHUNK), F32),
        ),
        name="prep",
    )(a_re, a_im, log_dt, b_re_t, b_im_t, w_s)


def _mixer_kernel(x_ref, gmix_ref, win_ref,
                  bblk_ref, are_ref, aim_ref, cblk_ref, dskip_ref, wglu_ref, bglu_ref,
                  wpa_ref, gsgu_ref, wsp_ref, bsb_ref, wpb_ref, wout_ref,
                  o_ref,
                  h_bf, st_re, st_im, bu_re, bu_im, u_scr, ya_scr):
    @pl.when(pl.program_id(0) == 0)
    def _():
        st_re[...] = jnp.zeros_like(st_re)
        st_im[...] = jnp.zeros_like(st_im)

    def x_slab(j):
        return x_ref[j * SLAB_BATCHES:(j + 1) * SLAB_BATCHES].reshape(SLAB, D_MODEL)

    def slab_rows(j):
        return slice(j * SLAB, (j + 1) * SLAB)

    n_slabs = ROWS // SLAB

    for j in range(n_slabs):
        h = _rms(x_slab(j), gmix_ref[...]).astype(BF16)
        h_bf[slab_rows(j), :] = h
        u = _dot(h, win_ref[:, WIN_U])
        for e in range(SLAB_BATCHES):
            for k in range(SSM_TILES):
                u_scr[k, pl.ds(j * SLAB_BATCHES + e, CHUNK, stride=BATCH), :] = (
                    u[e * CHUNK:(e + 1) * CHUNK, k * LANES:(k + 1) * LANES])

    def sub_rows(j):
        return slice(j * SCAN_ROWS, (j + 1) * SCAN_ROWS)

    def load_u(j):
        return _lane_tiles(u_scr, sub_rows(j), SSM_TILES)

    def project_in(j):
        slot = j % 2
        ub = load_u(j).astype(BF16)
        for k in range(N_BLK):
            bu = _dot(ub[:, k * U_BLK:(k + 1) * U_BLK], bblk_ref[k])
            bu_re[slot, :, k * ST_BLK:(k + 1) * ST_BLK] = bu[:, :ST_BLK]
            bu_im[slot, :, k * ST_BLK:(k + 1) * ST_BLK] = bu[:, ST_BLK:]

    def recur(j, state):
        slot = j % 2
        state = list(state)
        for t in range(SCAN_STEPS):
            rr = slice(t * BATCH, (t + 1) * BATCH)
            for cb in range(STATE_COLS // SCAN_COLS):
                cs = slice(cb * SCAN_COLS, (cb + 1) * SCAN_COLS)
                sre, sim = state[cb]
                ar = are_ref[:, cs]
                ai = aim_ref[:, cs]
                nre = ar * sre - ai * sim + bu_re[slot, rr, cs]
                nim = ar * sim + ai * sre + bu_im[slot, rr, cs]
                bu_re[slot, rr, cs] = nre
                bu_im[slot, rr, cs] = nim
                state[cb] = (nre, nim)
        return state

    def project_out(j):
        slot = j % 2
        ys = []
        for k in range(N_BLK):
            sr = bu_re[slot, :, k * ST_BLK:(k + 1) * ST_BLK].astype(BF16)
            si = bu_im[slot, :, k * ST_BLK:(k + 1) * ST_BLK].astype(BF16)
            ys.append(_dot(sr, cblk_ref[k, :ST_BLK, :]) + _dot(si, cblk_ref[k, ST_BLK:, :]))
        y = jnp.concatenate(ys, axis=1) + dskip_ref[...] * load_u(j)
        y = jax.nn.gelu(y)
        y = y * jax.nn.sigmoid(_dot(y.astype(BF16), wglu_ref[...]) + bglu_ref[...])
        for k in range(SSM_TILES):
            ya_scr[k, sub_rows(j), :] = y[:, k * LANES:(k + 1) * LANES]

    n_sub = CHUNK // SCAN_STEPS
    state = [(st_re[:, cb * SCAN_COLS:(cb + 1) * SCAN_COLS],
              st_im[:, cb * SCAN_COLS:(cb + 1) * SCAN_COLS])
             for cb in range(STATE_COLS // SCAN_COLS)]
    project_in(0)
    for j in range(n_sub):
        if j > 0:
            project_out(j - 1)
        if j + 1 < n_sub:
            project_in(j + 1)
        state = recur(j, state)
    project_out(n_sub - 1)
    for cb, (sre, sim) in enumerate(state):
        st_re[:, cb * SCAN_COLS:(cb + 1) * SCAN_COLS] = sre
        st_im[:, cb * SCAN_COLS:(cb + 1) * SCAN_COLS] = sim

    lane = lax.broadcasted_iota(jnp.int32, (CHUNK, LANES), 1)
    first_group = lane < SGU_GROUP_DIM

    def gate_dots(j):
        hs = h_bf[slab_rows(j), :]
        return (_dot(hs, win_ref[:, WIN_UV]), _dot(hs, win_ref[:, WIN_GB]),
                _dot(hs, win_ref[:, WIN_GA]))

    def merge(j, uv, gb, ga):
        uv = jax.nn.gelu(uv)
        v = _rms(uv[:, SGU_WIDTH:], gsgu_ref[...])
        tiles = []
        for k in range(SGU_TILES):
            rhs = []
            for e in range(SLAB_BATCHES):
                vt = v[e * CHUNK:(e + 1) * CHUNK, k * LANES:(k + 1) * LANES]
                rhs.append(jnp.concatenate([jnp.where(first_group, vt, 0.0).astype(BF16),
                                            jnp.where(first_group, 0.0, vt).astype(BF16)], axis=0))
            tiles.append(_dot(wsp_ref[k], jnp.concatenate(rhs, axis=1)))
        mixed = jnp.concatenate([
            jnp.concatenate([t[:, e * LANES:(e + 1) * LANES] for t in tiles], axis=1) + bsb_ref[...]
            for e in range(SLAB_BATCHES)], axis=0)
        yb = _dot((uv[:, :SGU_WIDTH] * mixed).astype(BF16), wpb_ref[...])
        ya_in = jnp.concatenate([
            _lane_tiles(ya_scr, pl.ds(j * SLAB_BATCHES + e, CHUNK, stride=BATCH), SSM_TILES)
            for e in range(SLAB_BATCHES)], axis=0)
        ya = _dot(ya_in.astype(BF16), wpa_ref[...])
        m = jax.nn.sigmoid(ga) * ya + jax.nn.sigmoid(gb) * yb
        x1 = x_slab(j) + _dot(m.astype(BF16), wout_ref[...])
        for e in range(SLAB_BATCHES):
            for k in range(D_TILES):
                o_ref[k, pl.ds(j * SLAB_BATCHES + e, CHUNK, stride=BATCH), :] = (
                    x1[e * CHUNK:(e + 1) * CHUNK, k * LANES:(k + 1) * LANES])

    nxt = gate_dots(0)
    for j in range(n_slabs):
        cur = nxt
        if j + 1 < n_slabs:
            nxt = gate_dots(j + 1)
        merge(j, *cur)


def _const_spec(shape):
    nd = len(shape)
    return pl.BlockSpec(shape, lambda i, nd=nd: (0,) * nd, pipeline_mode=pl.Buffered(1))


def _mixer(x, consts):
    return pl.pallas_call(
        _mixer_kernel,
        grid=(N_STEPS,),
        in_specs=[pl.BlockSpec((BATCH, CHUNK, D_MODEL), lambda i: (0, i, 0))]
        + [_const_spec(c.shape) for c in consts],
        out_specs=pl.BlockSpec((D_TILES, ROWS, LANES), lambda i: (0, i, 0)),
        out_shape=jax.ShapeDtypeStruct((D_TILES, SEQ * BATCH, LANES), F32),
        scratch_shapes=[
            pltpu.VMEM((ROWS, D_MODEL), BF16),
            pltpu.VMEM((BATCH, STATE_COLS), F32),
            pltpu.VMEM((BATCH, STATE_COLS), F32),
            pltpu.VMEM((2, SCAN_ROWS, STATE_COLS), F32),
            pltpu.VMEM((2, SCAN_ROWS, STATE_COLS), F32),
            pltpu.VMEM((SSM_TILES, ROWS, LANES), F32),
            pltpu.VMEM((SSM_TILES, ROWS, LANES), F32),
        ],
        compiler_params=pltpu.CompilerParams(
            dimension_semantics=("arbitrary",), vmem_limit_bytes=VMEM_LIMIT),
        name="mixer",
    )(x, *consts)


def _ffn_kernel(x_ref, gffn_ref, wup_ref, cw_ref, cb_ref, wd_ref, gfin_ref, o_ref,
                h_bf, acc, carry, fin_scr):
    @pl.when(pl.program_id(0) == 0)
    def _():
        carry[...] = jnp.zeros_like(carry)

    def slab_rows(j):
        return slice(j * FFN_SLAB, (j + 1) * FFN_SLAB)

    half_steps = HALF // BATCH
    last_tile = N_FF_TILES - 1

    def half_rows(r):
        return slice(r * HALF, (r + 1) * HALF)

    def cols_a(c):
        return slice(FF_BOUNDS[c], FF_BOUNDS[c + 1])

    def cols_b(c):
        return slice(D_FF + FF_BOUNDS[c], D_FF + FF_BOUNDS[c + 1])

    def norm_in(r):
        for j in range(HALF // FFN_SLAB):
            rows = slice(r * HALF + j * FFN_SLAB, r * HALF + (j + 1) * FFN_SLAB)
            h_bf[rows, :] = _rms(_lane_tiles(x_ref, rows, D_TILES), gffn_ref[...]).astype(BF16)

    def up(c, r):
        hb = h_bf[half_rows(r), :]
        return _dot(hb, wup_ref[:, cols_a(c)]), _dot(hb, wup_ref[:, cols_b(c)])

    def conv(y, prev, cols):
        y1 = jnp.concatenate([prev[BATCH:, :], y[:HALF - BATCH, :]], axis=0)
        y2 = jnp.concatenate([prev, y[:HALF - 2 * BATCH, :]], axis=0)
        return (y2 * cw_ref[0:1, cols] + y1 * cw_ref[1:2, cols] + y * cw_ref[2:3, cols]
                + cb_ref[:, cols])

    def norm_out(r, total):
        rows = half_rows(r)
        y = _rms(total, gfin_ref[...])
        for k in range(D_TILES):
            fin_scr[k, rows, :] = y[:, k * LANES:(k + 1) * LANES]
        for b in range(BATCH):
            for k in range(D_TILES):
                o_ref[b, r * half_steps:(r + 1) * half_steps, k * LANES:(k + 1) * LANES] = (
                    fin_scr[k, pl.ds(r * HALF + b, half_steps, stride=BATCH), :])

    order = []
    for s in range(N_FF_TILES + FFN_LAG):
        if s < N_FF_TILES:
            order.append((s, 0))
        if s >= FFN_LAG:
            order.append((s - FFN_LAG, 1))

    norm_in(0)
    nxt = up(0, 0)
    tails = {}
    for idx, (c, r) in enumerate(order):
        rows = half_rows(r)
        ya, yb = nxt
        if idx + 1 < len(order):
            c_next, r_next = order[idx + 1]
            if (c_next, r_next) == (0, 1):
                norm_in(1)
            nxt = up(c_next, r_next)
        if r == 0:
            prev_a, prev_b = carry[:, cols_a(c)], carry[:, cols_b(c)]
        else:
            prev_a, prev_b = tails[c]
            carry[:, cols_a(c)] = ya[HALF - 2 * BATCH:, :]
            carry[:, cols_b(c)] = yb[HALF - 2 * BATCH:, :]
        tails[c] = (ya[HALF - 2 * BATCH:, :], yb[HALF - 2 * BATCH:, :])
        a = conv(ya, prev_a, cols_a(c))
        b = conv(yb, prev_b, cols_b(c))
        down = _dot((jax.nn.silu(a) * b).astype(BF16), wd_ref[cols_a(c), :])
        if c == 0:
            acc[rows, :] = down
        elif c < last_tile:
            acc[rows, :] += down
        else:
            norm_out(r, _lane_tiles(x_ref, rows, D_TILES) + (acc[rows, :] + down))


def _ffn(x1, consts):
    return pl.pallas_call(
        _ffn_kernel,
        grid=(N_STEPS,),
        in_specs=[pl.BlockSpec((D_TILES, ROWS, LANES), lambda i: (0, i, 0))]
        + [_const_spec(c.shape) for c in consts],
        out_specs=pl.BlockSpec((BATCH, CHUNK, D_MODEL), lambda i: (0, i, 0)),
        out_shape=jax.ShapeDtypeStruct((BATCH, SEQ, D_MODEL), F32),
        scratch_shapes=[
            pltpu.VMEM((ROWS, D_MODEL), BF16),
            pltpu.VMEM((ROWS, D_MODEL), F32),
            pltpu.VMEM((2 * BATCH, 2 * D_FF), F32),
            pltpu.VMEM((D_TILES, ROWS, LANES), F32),
        ],
        compiler_params=pltpu.CompilerParams(
            dimension_semantics=("arbitrary",), vmem_limit_bytes=VMEM_LIMIT),
        name="ffn",
    )(x1, *consts)


def kernel(x, g_mix, w_in, a_re, a_im, log_dt, b_re, b_im, c_re, c_im, d_skip, w_glu, b_glu, w_proj_a, g_sgu, w_s, b_s, w_proj_b, w_out, g_ffn, w_up, conv_w, conv_b, w_down, g_final):
    l = 0
    abar_re, abar_im, bbar_re, bbar_im, ws_m = _prep(
        a_re[l], a_im[l], log_dt[l][:, None],
        b_re[l].transpose(0, 2, 1), b_im[l].transpose(0, 2, 1), w_s[l])
    are = jnp.broadcast_to(abar_re.reshape(1, STATE_COLS), (BATCH, STATE_COLS))
    aim = jnp.broadcast_to(abar_im.reshape(1, STATE_COLS), (BATCH, STATE_COLS))
    split = lambda re, im: jnp.stack([re, im]).reshape(2, N_BLK, GPB, SSM_GROUP, SSM_STATE)
    eye = jnp.eye(GPB, dtype=F32)
    bblk = jnp.einsum("rkghp,gq->kghrqp", split(bbar_re, bbar_im), eye).reshape(
        N_BLK, U_BLK, 2 * ST_BLK).astype(BF16)
    cblk = jnp.einsum("rkghp,gq->krgpqh", split(c_re[l], -c_im[l]), eye).reshape(
        N_BLK, 2 * ST_BLK, U_BLK).astype(BF16)
    wsp = ws_m.reshape(SGU_TILES, 2, CHUNK, CHUNK).transpose(0, 2, 1, 3).reshape(
        SGU_TILES, CHUNK, 2 * CHUNK).astype(BF16)
    bsb = jnp.repeat(b_s[l].T, SGU_GROUP_DIM, axis=1)
    mixer_consts = [
        g_mix[l][None, :], w_in[l].astype(BF16),
        bblk, are, aim, cblk, d_skip[l][None, :], w_glu[l].astype(BF16), b_glu[l][None, :],
        w_proj_a[l].astype(BF16), g_sgu[l][None, :], wsp, bsb,
        w_proj_b[l].astype(BF16), w_out[l].astype(BF16),
    ]
    ffn_consts = [
        g_ffn[l][None, :], w_up[l].astype(BF16), conv_w[l], conv_b[l][None, :],
        w_down[l].astype(BF16), g_final[None, :],
    ]
    x1 = _mixer(x, mixer_consts)
    return _ffn(x1, ffn_consts)
```
